```python
import math
import jax, jax.numpy as jnp
from jax import lax
import numpy as np

D_MODEL = 1024
BATCH = 8
SEQ = 2048
DEPTH = 2
DEC_BATCH = 128
DEC_SEQ = 4
PAST_LEN = 2048
PAGE_SIZE = 128

HEAD_DIM = 64
N_HEADS_A = 8
N_HEADS_B = 4
N_HEADS_C = 4
D_A = N_HEADS_A * HEAD_DIM
D_B = N_HEADS_B * HEAD_DIM
D_C = N_HEADS_C * HEAD_DIM
D_MIX = D_A + D_B + D_C
DILATED_PATTERNS = ((128, 1), (512, 4), (2048, 16))
MAX_WINDOW = 2048
Q_BLOCK = 128
CHUNK = 128
LORA_W = 32
LORA_A = 32
LORA_G = 64
D_C_PROJ = 3 * D_C + LORA_W + LORA_A + LORA_G
D_IN = 3 * D_A + 2 * D_B + D_C_PROJ
SPLIT_IN = (D_A, 2 * D_A, 3 * D_A, 3 * D_A + D_B, 3 * D_A + 2 * D_B)
SPLIT_C = (D_C, 2 * D_C, 3 * D_C, 3 * D_C + LORA_W, 3 * D_C + LORA_W + LORA_A)
N_GROUPS = 4
EXPERTS_PER_GROUP = 8
TOP_K_EXPERT = 2
D_EXPERT = 512
DEEPNORM_ALPHA = (2 * DEPTH) ** 0.25
DEEPNORM_BETA = (8 * DEPTH) ** -0.25
LN_EPS = 1e-5
GN_EPS = 64e-5
DECAY_SCALE = math.exp(-0.5)
L2_EPS = 1e-12

kernel_name = 'hybrid_dilated_sgu_rwkv7_hmoe_step'


def layer_norm(x, g, b):
    xf = x.astype(jnp.float32)
    mu = jnp.mean(xf, -1, keepdims=True)
    var = jnp.mean(jnp.square(xf - mu), -1, keepdims=True)
    return ((xf - mu) * lax.rsqrt(var + LN_EPS) * g.astype(jnp.float32) + b.astype(jnp.float32)).astype(x.dtype)


def dilated_attention(q, k, v, q_idx):
    qf = q.astype(jnp.float32) * (HEAD_DIM ** -0.5)
    outs, lses = [], []
    for window, dilation in DILATED_PATTERNS:
        n_keys = window // dilation + 1
        idx = q_idx[:, None] - dilation * jnp.arange(n_keys, dtype=jnp.int32)[None, :]
        valid = idx >= 0
        idx = jnp.maximum(idx, 0)
        kg = k[:, idx].astype(jnp.float32)
        vg = v[:, idx].astype(jnp.float32)
        s = jnp.einsum('bqhd,bqnhd->bhqn', qf, kg)
        s = jnp.where(valid[None, None], s, -jnp.inf)
        m = jnp.max(s, -1, keepdims=True)
        pr = jnp.exp(s - m)
        den = jnp.sum(pr, -1, keepdims=True)
        outs.append(jnp.einsum('bhqn,bqnhd->bhqd', pr, vg) / den)
        lses.append((m + jnp.log(den))[..., 0])
    wts = jax.nn.softmax(jnp.stack(lses, 0), axis=0)
    o = jnp.einsum('gbhq,gbhqd->bqhd', wts, jnp.stack(outs, 0))
    return o.astype(q.dtype)


def dilated_attention_prompt(q, k, v):
    B, T, H, D = q.shape
    nb = T // Q_BLOCK
    qb = jnp.swapaxes(q.reshape(B, nb, Q_BLOCK, H, D), 0, 1)
    starts = jnp.arange(nb, dtype=jnp.int32) * Q_BLOCK

    def one_block(args):
        q_blk, s0 = args
        return dilated_attention(q_blk, k, v, s0 + jnp.arange(Q_BLOCK, dtype=jnp.int32))

    o = lax.map(one_block, (qb, starts))
    return jnp.swapaxes(o, 0, 1).reshape(B, T, H, D)


def spatial_gating(u, vn, ws, bias):
    B, T, _ = u.shape
    c = min(T, CHUNK)
    nc = T // c
    mask = jnp.tril(jnp.ones((c, c), dtype=bool))
    wm = jnp.where(mask, ws[:, :c, :c], 0.0).astype(jnp.float32)
    vh = vn.reshape(B, nc, c, N_HEADS_B, HEAD_DIM).astype(jnp.float32)
    z = jnp.einsum('hts,bcshd->bcthd', wm, vh) + jnp.swapaxes(bias[:, :c], 0, 1).astype(jnp.float32)[None, None, :, :, None]
    return u * z.reshape(B, T, D_B).astype(u.dtype)


def rwkv_time_mix(pc, shift0, wkv0, p):
    B, T, _ = pc.shape
    f32 = jnp.float32
    prev = jnp.concatenate([shift0[:, None, :].astype(pc.dtype), pc[:, :-1]], axis=1)
    xs = (pc + (prev - pc) * p['rwkv_mu']).astype(f32)
    r, k, v, wd, ad, gd = jnp.split(xs, SPLIT_C, axis=-1)
    decay = jnp.exp(-DECAY_SCALE * jax.nn.sigmoid(p['rwkv_w0'].astype(f32) + jnp.tanh(wd) @ p['rwkv_w2'].astype(f32)))
    a = jax.nn.sigmoid(p['rwkv_a0'].astype(f32) + ad @ p['rwkv_a2'].astype(f32))
    g = jax.nn.sigmoid(gd) @ p['rwkv_g2'].astype(f32)
    hs = lambda t: t.reshape(B, T, N_HEADS_C, HEAD_DIM)
    kk = hs(k * p['rwkv_k_k'].astype(f32))
    kk = kk / jnp.maximum(jnp.sqrt(jnp.sum(kk * kk, -1, keepdims=True)), L2_EPS)
    k = k * (1.0 + (a - 1.0) * p['rwkv_k_a'].astype(f32))
    r, k, v, decay, a = hs(r), hs(k), hs(v), hs(decay), hs(a)

    def step(S, inp):
        r_t, w_t, k_t, v_t, kk_t, b_t = inp
        sa = jnp.einsum('bhij,bhj->bhi', S, kk_t)
        S = S * w_t[:, :, None, :] - sa[..., None] * b_t[:, :, None, :] + v_t[..., None] * k_t[:, :, None, :]
        return S, jnp.einsum('bhij,bhj->bhi', S, r_t)

    tm = lambda t: jnp.swapaxes(t, 0, 1)
    S_T, y = lax.scan(step, wkv0.astype(f32), (tm(r), tm(decay), tm(k), tm(v), tm(kk), tm(kk * a)))
    y = tm(y)
    mu = jnp.mean(y, -1, keepdims=True)
    var = jnp.mean(jnp.square(y - mu), -1, keepdims=True)
    y = ((y - mu) * lax.rsqrt(var + GN_EPS)).reshape(B, T, D_C) * p['rwkv_lnx_g'].astype(f32) + p['rwkv_lnx_b'].astype(f32)
    bonus = (jnp.sum(r * k * p['rwkv_r_k'].astype(f32), -1, keepdims=True) * v).reshape(B, T, D_C)
    out = (y + bonus) * g
    return out.astype(pc.dtype), S_T.astype(pc.dtype), pc[:, -1]


def hierarchical_moe(x, p):
    B, T, D = x.shape
    f32 = jnp.float32
    xt = x.reshape(B * T, D)
    logit_g = (xt @ p['moe_router1']).astype(f32) + p['moe_router1_b'].astype(f32)
    prob_g = jax.nn.softmax(logit_g, -1)
    grp = jnp.argmax(logit_g, -1)
    gate_g = jnp.take_along_axis(prob_g, grp[:, None], 1)
    logit_e = (xt @ p['moe_router2']).astype(f32).reshape(-1, N_GROUPS, EXPERTS_PER_GROUP) + p['moe_router2_b'].astype(f32)
    logit_e = jnp.take_along_axis(logit_e, grp[:, None, None], 1)[:, 0]
    top_val, top_idx = lax.top_k(logit_e, TOP_K_EXPERT)
    top_w = jax.nn.softmax(top_val, -1)
    w_e = jnp.einsum('nk,nke->ne', top_w, jax.nn.one_hot(top_idx, EXPERTS_PER_GROUP, dtype=f32))
    comb = (gate_g * w_e)[:, None, :] * jax.nn.one_hot(grp, N_GROUPS, dtype=f32)[:, :, None]
    y = jnp.zeros_like(xt)
    for gi in range(N_GROUPS):
        h = jax.nn.silu(jnp.einsum('nd,edf->nef', xt, p['moe_w_gate'][gi])) * jnp.einsum('nd,edf->nef', xt, p['moe_w_up'][gi])
        h = h * comb[:, gi, :, None].astype(h.dtype)
        y = y + jnp.einsum('nef,efd->nd', h, p['moe_w_down'][gi]).astype(xt.dtype)
    return y.reshape(B, T, D)


def decoder_layer(x, p, win_k, win_v, wkv0, shift0):
    B, T, _ = x.shape
    proj = jnp.einsum('btd,de->bte', x, p['w_in'])
    qa, ka, va, ub, vb, pc = jnp.split(proj, SPLIT_IN, axis=-1)
    heads = lambda t: t.reshape(B, T, N_HEADS_A, HEAD_DIM)
    q, k, v = heads(qa), heads(ka), heads(va)
    if win_k is None:
        o_a = dilated_attention_prompt(q, k, v)
    else:
        k_all = jnp.concatenate([win_k.astype(k.dtype), k], axis=1)
        v_all = jnp.concatenate([win_v.astype(v.dtype), v], axis=1)
        q_idx = win_k.shape[1] + jnp.arange(T, dtype=jnp.int32)
        o_a = dilated_attention(q, k_all, v_all, q_idx)
    o_a = o_a.reshape(B, T, D_A)
    u = jax.nn.gelu(ub, approximate=False)
    vn = layer_norm(jax.nn.gelu(vb, approximate=False), p['sgu_ln_g'], p['sgu_ln_b'])
    o_b = spatial_gating(u, vn, p['sgu_ws'], p['sgu_bias'])
    o_c, wkv_T, shift_T = rwkv_time_mix(pc, shift0, wkv0, p)
    mix = jnp.einsum('bte,ed->btd', jnp.concatenate([o_a, o_b, o_c], -1), p['w_out'])
    x = layer_norm(DEEPNORM_ALPHA * x + mix, p['ln1_g'], p['ln1_b'])
    x = layer_norm(DEEPNORM_ALPHA * x + hierarchical_moe(x, p), p['ln2_g'], p['ln2_b'])
    return x, k, v, wkv_T, shift_T, vn


def setup_inputs(seed: int = 0) -> dict:
    key = jax.random.key(seed)
    keys = jax.random.split(key, 48)
    counter = [0]

    def nk():
        counter[0] += 1
        return keys[counter[0] - 1]

    def nrm(shape, scale):
        return jax.random.normal(nk(), shape, jnp.float32) * scale

    L = DEPTH
    w_buf = min(MAX_WINDOW, PAST_LEN)
    return {
        'x_prompt': nrm((BATCH, SEQ, D_MODEL), 1.0),
        'x_sample': nrm((DEC_BATCH, DEC_SEQ, D_MODEL), 1.0),
        'cache_win_k': nrm((L, DEC_BATCH, w_buf, N_HEADS_A, HEAD_DIM), 1.0),
        'cache_win_v': nrm((L, DEC_BATCH, w_buf, N_HEADS_A, HEAD_DIM), 1.0),
        'state_wkv': nrm((L, DEC_BATCH, N_HEADS_C, HEAD_DIM, HEAD_DIM), 0.3),
        'state_shift': nrm((L, DEC_BATCH, D_C_PROJ), 1.0),
        'w_in': nrm((L, D_MODEL, D_IN), D_MODEL ** -0.5),
        'w_out': nrm((L, D_MIX, D_MODEL), DEEPNORM_BETA * D_MIX ** -0.5),
        'sgu_ln_g': 1.0 + nrm((L, D_B), 0.02),
        'sgu_ln_b': nrm((L, D_B), 0.02),
        'sgu_ws': nrm((L, N_HEADS_B, CHUNK, CHUNK), CHUNK ** -0.5),
        'sgu_bias': 1.0 + nrm((L, N_HEADS_B, CHUNK), 0.1),
        'rwkv_mu': jax.random.uniform(nk(), (L, D_C_PROJ), jnp.float32),
        'rwkv_w0': nrm((L, D_C), 0.5),
        'rwkv_w2': nrm((L, LORA_W, D_C), 0.5 * LORA_W ** -0.5),
        'rwkv_a0': nrm((L, D_C), 0.5),
        'rwkv_a2': nrm((L, LORA_A, D_C), 0.5 * LORA_A ** -0.5),
        'rwkv_g2': nrm((L, LORA_G, D_C), LORA_G ** -0.5),
        'rwkv_k_k': 0.85 + nrm((L, D_C), 0.05),
        'rwkv_k_a': 1.0 + nrm((L, D_C), 0.05),
        'rwkv_r_k': nrm((L, N_HEADS_C, HEAD_DIM), 0.1),
        'rwkv_lnx_g': 1.0 + nrm((L, D_C), 0.02),
        'rwkv_lnx_b': nrm((L, D_C), 0.02),
        'ln1_g': 1.0 + nrm((L, D_MODEL), 0.02),
        'ln1_b': nrm((L, D_MODEL), 0.02),
        'ln2_g': 1.0 + nrm((L, D_MODEL), 0.02),
        'ln2_b': nrm((L, D_MODEL), 0.02),
        'moe_router1': nrm((L, D_MODEL, N_GROUPS), D_MODEL ** -0.5),
        'moe_router1_b': nrm((L, N_GROUPS), 0.01),
        'moe_router2': nrm((L, D_MODEL, N_GROUPS * EXPERTS_PER_GROUP), D_MODEL ** -0.5),
        'moe_router2_b': nrm((L, N_GROUPS, EXPERTS_PER_GROUP), 0.01),
        'moe_w_gate': nrm((L, N_GROUPS, EXPERTS_PER_GROUP, D_MODEL, D_EXPERT), D_MODEL ** -0.5),
        'moe_w_up': nrm((L, N_GROUPS, EXPERTS_PER_GROUP, D_MODEL, D_EXPERT), D_MODEL ** -0.5),
        'moe_w_down': nrm((L, N_GROUPS, EXPERTS_PER_GROUP, D_EXPERT, D_MODEL), DEEPNORM_BETA * D_EXPERT ** -0.5),
    }


def reference(x_prompt, x_sample, cache_win_k, cache_win_v, state_wkv, state_shift,
              w_in, w_out, sgu_ln_g, sgu_ln_b, sgu_ws, sgu_bias,
              rwkv_mu, rwkv_w0, rwkv_w2, rwkv_a0, rwkv_a2, rwkv_g2, rwkv_k_k, rwkv_k_a, rwkv_r_k,
              rwkv_lnx_g, rwkv_lnx_b, ln1_g, ln1_b, ln2_g, ln2_b,
              moe_router1, moe_router1_b, moe_router2, moe_router2_b, moe_w_gate, moe_w_up, moe_w_down):
    bp, tp, _ = x_prompt.shape
    keep = min(MAX_WINDOW, tp)
    wkv_zero = jnp.zeros((bp, N_HEADS_C, HEAD_DIM, HEAD_DIM), x_prompt.dtype)
    shift_zero = jnp.zeros((bp, D_C_PROJ), x_prompt.dtype)
    xp, xs = x_prompt, x_sample
    kp_l, vp_l, wkvp_l, shp_l = [], [], [], []
    ks_l, vs_l, wkvs_l, shs_l, sgu_l = [], [], [], [], []
    for l in range(DEPTH):
        p = dict(w_in=w_in[l], w_out=w_out[l], sgu_ln_g=sgu_ln_g[l], sgu_ln_b=sgu_ln_b[l],
                 sgu_ws=sgu_ws[l], sgu_bias=sgu_bias[l], rwkv_mu=rwkv_mu[l], rwkv_w0=rwkv_w0[l],
                 rwkv_w2=rwkv_w2[l], rwkv_a0=rwkv_a0[l], rwkv_a2=rwkv_a2[l], rwkv_g2=rwkv_g2[l],
                 rwkv_k_k=rwkv_k_k[l], rwkv_k_a=rwkv_k_a[l], rwkv_r_k=rwkv_r_k[l],
                 rwkv_lnx_g=rwkv_lnx_g[l], rwkv_lnx_b=rwkv_lnx_b[l], ln1_g=ln1_g[l], ln1_b=ln1_b[l],
                 ln2_g=ln2_g[l], ln2_b=ln2_b[l], moe_router1=moe_router1[l], moe_router1_b=moe_router1_b[l],
                 moe_router2=moe_router2[l], moe_router2_b=moe_router2_b[l], moe_w_gate=moe_w_gate[l],
                 moe_w_up=moe_w_up[l], moe_w_down=moe_w_down[l])
        xp, kp, vp, wkvp, shp, _ = decoder_layer(xp, p, None, None, wkv_zero, shift_zero)
        kp_l.append(kp[:, tp - keep:]); vp_l.append(vp[:, tp - keep:]); wkvp_l.append(wkvp); shp_l.append(shp)
        xs, ks, vs, wkvs, shs, vns = decoder_layer(xs, p, cache_win_k[l], cache_win_v[l], state_wkv[l], state_shift[l])
        ks_l.append(ks); vs_l.append(vs); wkvs_l.append(wkvs); shs_l.append(shs); sgu_l.append(vns)
    return (xp, xs,
            jnp.stack(kp_l, 0), jnp.stack(vp_l, 0), jnp.stack(wkvp_l, 0), jnp.stack(shp_l, 0),
            jnp.stack(ks_l, 0), jnp.stack(vs_l, 0), jnp.stack(wkvs_l, 0), jnp.stack(shs_l, 0), jnp.stack(sgu_l, 0))
```

```python
import functools
import math

import numpy as np
import jax
import jax.numpy as jnp
from jax import lax
from jax.experimental import pallas as pl
from jax.experimental.pallas import tpu as pltpu

F32 = jnp.float32
BF16 = jnp.bfloat16

D_MODEL = 1024
HEAD_DIM = 64
N_HEADS_A = 8
N_HEADS_B = 4
N_HEADS_C = 4
D_A = N_HEADS_A * HEAD_DIM
D_B = N_HEADS_B * HEAD_DIM
D_C = N_HEADS_C * HEAD_DIM
DILATED_PATTERNS = ((128, 1), (512, 4), (2048, 16))
CHUNK = 128
LORA_W, LORA_A, LORA_G = 32, 32, 64
D_LORA = LORA_W + LORA_A + LORA_G
D_C_PROJ = 3 * D_C + D_LORA
D_QKV = 3 * D_A
D_UV = 2 * D_B
D_IN = D_QKV + D_UV + D_C_PROJ
N_GROUPS = 4
EXPERTS_PER_GROUP = 8
N_EXPERTS = N_GROUPS * EXPERTS_PER_GROUP
D_EXPERT = 512
DEPTH = 2
DEEPNORM_ALPHA = (2 * DEPTH) ** 0.25
LN_EPS = 1e-5
GN_EPS = 64e-5
DECAY_SCALE = math.exp(-0.5)
L2_EPS = 1e-12
NEG = -1e30

LANES = 128
ROW_TILE = 512
Q_BLOCK = 128
RWKV_CHUNK = 64
MOE_TILE = 256
VMEM_LIMIT = 56 * 1024 * 1024


def _cparams(*sem):
    return pltpu.CompilerParams(dimension_semantics=sem, vmem_limit_bytes=VMEM_LIMIT)


def _dg(a, b, ca=1, cb=0):
    return lax.dot_general(a, b, (((ca,), (cb,)), ((), ())), preferred_element_type=F32)


def _split(x):
    hi = x.astype(BF16)
    lo = (x - hi.astype(F32)).astype(BF16)
    return hi, lo


def _dot3(a, b, ca=1, cb=0):
    ah, al = _split(a)
    bh, bl = _split(b)
    return _dg(ah, bh, ca, cb) + (_dg(ah, bl, ca, cb) + _dg(al, bh, ca, cb))


def _dot_exact_lhs(a_bf16, b, ca=1, cb=0):
    bh, bl = _split(b)
    return _dg(a_bf16, bh, ca, cb) + _dg(a_bf16, bl, ca, cb)


def _dot_exact_rhs(a, b_bf16, ca=1, cb=0):
    ah, al = _split(a)
    return _dg(ah, b_bf16, ca, cb) + _dg(al, b_bf16, ca, cb)


def _sigmoid(x):
    return 1.0 / (1.0 + jnp.exp(-x))


def _gelu(x):
    return 0.5 * x * (1.0 + lax.erf(x * (2.0 ** -0.5)))


def _layer_norm(z, g, b):
    mu = jnp.mean(z, axis=-1, keepdims=True)
    d = z - mu
    var = jnp.mean(d * d, axis=-1, keepdims=True)
    return d * lax.rsqrt(var + LN_EPS) * g + b


def _in_proj_kernel(x_ref, w_ref, qkv_ref, uv_ref, pc_ref):
    x = x_ref[...].astype(BF16)
    qkv_ref[...] = jnp.dot(x, w_ref[:, 0:D_QKV], preferred_element_type=F32)
    uv_ref[...] = jnp.dot(x, w_ref[:, D_QKV:D_QKV + D_UV], preferred_element_type=F32)
    pc_ref[...] = jnp.dot(x, w_ref[:, D_QKV + D_UV:D_IN], preferred_element_type=F32)


def _in_proj(x, w_in_bf16, layer):
    n = x.shape[0]
    row = lambda w: pl.BlockSpec((ROW_TILE, w), lambda i: (i, 0))
    return pl.pallas_call(
        _in_proj_kernel,
        grid=(n // ROW_TILE,),
        in_specs=[row(D_MODEL), pl.BlockSpec((None, D_MODEL, D_IN), lambda i: (layer, 0, 0))],
        out_specs=[row(D_QKV), row(D_UV), row(D_C_PROJ)],
        out_shape=[jax.ShapeDtypeStruct((n, w), F32) for w in (D_QKV, D_UV, D_C_PROJ)],
        compiler_params=_cparams("parallel"),
        name="in_proj",
    )(x, w_in_bf16)


def _pattern_count(delta):
    c = np.zeros(delta.shape, np.float64)
    for window, dilation in DILATED_PATTERNS:
        c += (delta >= 0) & (delta <= window) & (delta % dilation == 0)
    return c


def _log_count(delta):
    c = _pattern_count(delta)
    return np.where(c > 0, np.log(np.maximum(c, 1.0)), NEG).astype(np.float32)


def _prompt_bias_table(nb):
    i = np.arange(Q_BLOCK)[:, None]
    j = np.arange(Q_BLOCK)[None, :]
    tab = np.stack([_log_count(d * Q_BLOCK + i - j) for d in range(nb)], 0)
    return np.concatenate([tab, tab], axis=1)


def _attn_prompt_kernel(q_ref, k_ref, v_ref, bias_ref, o_ref, kb_ref, vb_ref, *, nb):
    kb_ref[...] = k_ref[...].astype(BF16)
    vb_ref[...] = v_ref[...].astype(BF16)
    first = lax.broadcasted_iota(jnp.int32, (Q_BLOCK, LANES), 1) < HEAD_DIM
    scale = HEAD_DIM ** -0.5

    def q_block(qb, _):
        q0 = pl.multiple_of(qb * Q_BLOCK, Q_BLOCK)
        q = q_ref[pl.ds(q0, Q_BLOCK), :] * scale
        q2 = jnp.concatenate([jnp.where(first, q, 0.0), jnp.where(first, 0.0, q)], axis=0).astype(BF16)

        def k_block(kb, carry):
            m, l, acc = carry
            k0 = pl.multiple_of(kb * Q_BLOCK, Q_BLOCK)
            s = _dg(q2, kb_ref[pl.ds(k0, Q_BLOCK), :], 1, 1) + bias_ref[qb - kb]
            m_new = jnp.maximum(m, jnp.max(s, axis=-1, keepdims=True))
            alpha = jnp.exp(m - m_new)
            p = jnp.exp(s - m_new)
            l = alpha * l + jnp.sum(p, axis=-1, keepdims=True)
            acc = alpha * acc + jnp.dot(p.astype(BF16), vb_ref[pl.ds(k0, Q_BLOCK), :], preferred_element_type=F32)
            return m_new, l, acc

        init = (jnp.full((2 * Q_BLOCK, 1), NEG, F32), jnp.zeros((2 * Q_BLOCK, 1), F32),
                jnp.zeros((2 * Q_BLOCK, LANES), F32))
        _, l, acc = lax.fori_loop(0, qb + 1, k_block, init)
        o = acc / l
        o_ref[pl.ds(q0, Q_BLOCK), :] = jnp.where(first, o[:Q_BLOCK], o[Q_BLOCK:])
        return 0

    lax.fori_loop(0, nb, q_block, 0)


def _attn_prompt(qkv, batch, seq):
    nb = seq // Q_BLOCK
    pairs = D_A // LANES
    bias = jnp.asarray(_prompt_bias_table(nb))
    blk = lambda off: pl.BlockSpec((seq, LANES), lambda b, p: (b, off + p))
    return pl.pallas_call(
        functools.partial(_attn_prompt_kernel, nb=nb),
        grid=(batch, pairs),
        in_specs=[blk(0), blk(pairs), blk(2 * pairs),
                  pl.BlockSpec((nb, 2 * Q_BLOCK, Q_BLOCK), lambda b, p: (0, 0, 0))],
        out_specs=pl.BlockSpec((seq, LANES), lambda b, p: (b, p)),
        out_shape=jax.ShapeDtypeStruct((batch * seq, D_A), F32),
        scratch_shapes=[pltpu.VMEM((seq, LANES), BF16), pltpu.VMEM((seq, LANES), BF16)],
        compiler_params=_cparams("parallel", "parallel"),
        name="attn_prompt",
    )(qkv, qkv, qkv, bias)


def _sgu_prompt_kernel(u_ref, v_ref, ws_ref, bias_ref, g_ref, b_ref, o_ref):
    u = _gelu(u_ref[...])
    vn = _layer_norm(_gelu(v_ref[...]), g_ref[...], b_ref[...]).astype(BF16)
    r = lax.broadcasted_iota(jnp.int32, (CHUNK, CHUNK), 0)
    c = lax.broadcasted_iota(jnp.int32, (CHUNK, CHUNK), 1)
    head = lax.broadcasted_iota(jnp.int32, (CHUNK, D_B), 1) // HEAD_DIM
    z = bias_ref[...]
    for h in range(N_HEADS_B):
        wm = jnp.where(r >= c, ws_ref[h], 0.0).astype(BF16)
        z = z + jnp.where(head == h, jnp.dot(wm, vn, preferred_element_type=F32), 0.0)
    o_ref[...] = u * z


def _expand_sgu_bias(bias):
    return jnp.repeat(jnp.swapaxes(bias, 1, 2), HEAD_DIM, axis=2)


def _sgu_prompt(uv, n, ws, bias_full, g, b, layer):
    vec = pl.BlockSpec((None, 1, D_B), lambda i: (layer, 0, 0))
    return pl.pallas_call(
        _sgu_prompt_kernel,
        grid=(n // CHUNK,),
        in_specs=[pl.BlockSpec((CHUNK, D_B), lambda i: (i, 0)), pl.BlockSpec((CHUNK, D_B), lambda i: (i, 1)),
                  pl.BlockSpec((None, N_HEADS_B, CHUNK, CHUNK), lambda i: (layer, 0, 0, 0)),
                  pl.BlockSpec((None, CHUNK, D_B), lambda i: (layer, 0, 0)), vec, vec],
        out_specs=pl.BlockSpec((CHUNK, D_B), lambda i: (i, 0)),
        out_shape=jax.ShapeDtypeStruct((n, D_B), F32),
        compiler_params=_cparams("parallel"),
        name="sgu_prompt",
    )(uv, uv, ws, bias_full, g, b)


def _rwkv_lora_weights(w2, a2, g2):
    depth = w2.shape[0]
    w = jnp.zeros((depth, D_LORA, 3 * D_C), F32)
    w = w.at[:, 0:LORA_W, 0:D_C].set(w2)
    w = w.at[:, LORA_W:LORA_W + LORA_A, D_C:2 * D_C].set(a2)
    return w.at[:, LORA_W + LORA_A:, 2 * D_C:].set(g2)


def _rwkv_features(xs, wl, vec):
    r = xs[:, 0:D_C]
    k = xs[:, D_C:2 * D_C]
    v = xs[:, 2 * D_C:3 * D_C]
    tail = xs[:, 3 * D_C:]
    lane = lax.broadcasted_iota(jnp.int32, tail.shape, 1)
    feats = jnp.where(lane < LORA_W, jnp.tanh(tail), jnp.where(lane < LORA_W + LORA_A, tail, _sigmoid(tail)))
    lora = _dot3(feats, wl)
    logw = -DECAY_SCALE * _sigmoid(vec["w0"] + lora[:, 0:D_C])
    a = _sigmoid(vec["a0"] + lora[:, D_C:2 * D_C])
    gate = lora[:, 2 * D_C:]
    kk = k * vec["k_k"]
    ri = lax.broadcasted_iota(jnp.int32, (D_C, D_C), 0) // HEAD_DIM
    ci = lax.broadcasted_iota(jnp.int32, (D_C, D_C), 1) // HEAD_DIM
    head_ones = jnp.where(ri == ci, 1.0, 0.0).astype(BF16)
    kk = kk / jnp.maximum(jnp.sqrt(_dot_exact_rhs(kk * kk, head_ones)), L2_EPS)
    k = k * (1.0 + (a - 1.0) * vec["k_a"])
    bonus = _dot_exact_rhs(r * k * vec["r_k"], head_ones) * v
    return r, k, v, logw, kk, kk * a, gate, bonus


def _rwkv_chunk_fn(C, n_valid, wl_ref, vec_ref):
    H = N_HEADS_C
    names = ("mu_r", "mu_k", "mu_v", "w0", "a0", "k_k", "k_a", "r_k", "lnx_g", "lnx_b")
    vec = {n: vec_ref[i:i + 1, :] for i, n in enumerate(names)}
    mu = jnp.concatenate([vec["mu_r"], vec["mu_k"], vec["mu_v"], vec_ref[len(names):len(names) + 1, 0:D_LORA]], axis=1)

    row = lax.broadcasted_iota(jnp.int32, (C, D_C_PROJ), 0)
    is_token = lax.broadcasted_iota(jnp.int32, (C, D_C), 0) < n_valid
    srow = lax.broadcasted_iota(jnp.int32, (H * C, H * C), 0)
    scol = lax.broadcasted_iota(jnp.int32, (H * C, H * C), 1)
    strict = (srow % C) > (scol % C)
    incl = (srow % C) >= (scol % C)
    eye = jnp.where(srow == scol, 1.0, 0.0)
    stack_head = lax.broadcasted_iota(jnp.int32, (H * C, D_C), 0) // C
    stack_lane_head = lax.broadcasted_iota(jnp.int32, (H * C, D_C), 1) // HEAD_DIM
    own = stack_head == stack_lane_head
    tr = lax.broadcasted_iota(jnp.int32, (C, C), 0)
    tc = lax.broadcasted_iota(jnp.int32, (C, C), 1)
    cum = jnp.where(tr >= tc, 1.0, 0.0).astype(BF16)

    def stack(x):
        return jnp.where(own, jnp.concatenate([x] * H, axis=0), 0.0)

    def unstack(xs):
        return (xs[0:C] + xs[C:2 * C]) + (xs[2 * C:3 * C] + xs[3 * C:4 * C])

    def chunk(pc, prev_row, s0):
        prev = jnp.where(row == 0, prev_row, pltpu.roll(pc, 1, axis=0))
        xs = pc + (prev - pc) * mu
        r, k, v, logw, kk, b, gate, bonus = _rwkv_features(xs, wl_ref[...], vec)
        if n_valid < C:
            logw = jnp.where(is_token, logw, 0.0)
            kk = jnp.where(is_token, kk, 0.0)
            b = jnp.where(is_token, b, 0.0)
            k = jnp.where(is_token, k, 0.0)

        lc = _dot_exact_lhs(cum, logw)
        lc_end = lc[C - 1:C, :]
        w_inv = jnp.exp(-lc)
        a_t = stack(-kk * jnp.exp(lc - logw))
        r_t = stack(r * jnp.exp(lc))
        b_t = stack(b * w_inv)
        k_t = stack(k * w_inv)
        to_end = jnp.exp(lc_end - lc)
        b_e = stack(b * to_end)
        k_e = stack(k * to_end)
        v_s = stack(v)

        ab = jnp.where(strict, _dot3(a_t, b_t, 1, 1), 0.0)
        ak = jnp.where(strict, _dot3(a_t, k_t, 1, 1), 0.0)
        rb = jnp.where(incl, _dg(r_t.astype(BF16), b_t.astype(BF16), 1, 1), 0.0)
        rk = jnp.where(incl, _dg(r_t.astype(BF16), k_t.astype(BF16), 1, 1), 0.0)

        inv = eye + ab
        power = ab
        for _ in range(int(math.log2(C)) - 1):
            power = _dot3(power, power)
            inv = inv + _dot3(power, inv)

        s0_b = s0.astype(BF16)
        rhs = _dg(a_t.astype(BF16), s0_b, 1, 1) + _dot3(ak, v_s)
        u = _dot3(inv, rhs)
        u_b = u.astype(BF16)
        v_b = v_s.astype(BF16)
        y = unstack(_dg(r_t.astype(BF16), s0_b, 1, 1) + _dg(rb.astype(BF16), u_b) + _dg(rk.astype(BF16), v_b))
        s_end = s0 * jnp.exp(lc_end) + _dg(u_b, b_e.astype(BF16), 0, 0) + _dg(v_b, k_e.astype(BF16), 0, 0)

        ys = stack(y)
        mean = jnp.sum(ys, axis=-1, keepdims=True) * (1.0 / HEAD_DIM)
        d = jnp.where(own, ys - mean, 0.0)
        var = jnp.sum(d * d, axis=-1, keepdims=True) * (1.0 / HEAD_DIM)
        yn = unstack(d * lax.rsqrt(var + GN_EPS)) * vec["lnx_g"] + vec["lnx_b"]
        return (yn + bonus) * gate, s_end

    return chunk


def _store_head_states(s_out_ref, s):
    for h in range(N_HEADS_C):
        s_out_ref[h] = s[h * HEAD_DIM:(h + 1) * HEAD_DIM, h * HEAD_DIM:(h + 1) * HEAD_DIM]


def _rwkv_prompt_kernel(pc_ref, wl_ref, vec_ref, o_ref, s_out_ref, s_ref, prev_ref):
    C = RWKV_CHUNK
    chunk = _rwkv_chunk_fn(C, C, wl_ref, vec_ref)
    s_ref[...] = jnp.zeros_like(s_ref)
    prev_ref[...] = jnp.zeros_like(prev_ref)

    def step(c, _):
        t0 = pl.multiple_of(c * C, C)
        pc = pc_ref[pl.ds(t0, C), :]
        out, s_end = chunk(pc, prev_ref[...], s_ref[...])
        prev_ref[...] = pc[C - 1:C, :]
        s_ref[...] = s_end
        o_ref[pl.ds(t0, C), :] = out
        return 0

    lax.fori_loop(0, pc_ref.shape[0] // C, step, 0)
    _store_head_states(s_out_ref, s_ref[...])


def _rwkv_sample_kernel(pc_ref, shift_ref, s_in_ref, wl_ref, vec_ref, o_ref, s_out_ref, *, n_valid):
    C = pc_ref.shape[0]
    zero = jnp.zeros((HEAD_DIM, HEAD_DIM), F32)
    s0 = jnp.concatenate(
        [jnp.concatenate([s_in_ref[h] if g == h else zero for g in range(N_HEADS_C)], axis=1)
         for h in range(N_HEADS_C)], axis=0)
    out, s_end = _rwkv_chunk_fn(C, n_valid, wl_ref, vec_ref)(pc_ref[...], shift_ref[...], s0)
    o_ref[...] = out
    _store_head_states(s_out_ref, s_end)


def _rwkv_sample(pc_pad, shift0, wkv0, wl, vec, layer, n_valid):
    batch, rows, _ = pc_pad.shape
    state = pl.BlockSpec((None, None, N_HEADS_C, HEAD_DIM, HEAD_DIM), lambda b: (layer, b, 0, 0, 0))
    return pl.pallas_call(
        functools.partial(_rwkv_sample_kernel, n_valid=n_valid),
        grid=(batch,),
        in_specs=[pl.BlockSpec((None, rows, D_C_PROJ), lambda b: (b, 0, 0)),
                  pl.BlockSpec((None, None, 1, D_C_PROJ), lambda b: (layer, b, 0, 0)),
                  state,
                  pl.BlockSpec((D_LORA, 3 * D_C), lambda b: (0, 0)),
                  pl.BlockSpec((16, D_C), lambda b: (0, 0))],
        out_specs=[pl.BlockSpec((None, rows, D_C), lambda b: (b, 0, 0)),
                   pl.BlockSpec((None, N_HEADS_C, HEAD_DIM, HEAD_DIM), lambda b: (b, 0, 0, 0))],
        out_shape=[jax.ShapeDtypeStruct((batch, rows, D_C), F32),
                   jax.ShapeDtypeStruct((batch, N_HEADS_C, HEAD_DIM, HEAD_DIM), F32)],
        compiler_params=_cparams("parallel"),
        name="rwkv_sample",
    )(pc_pad, shift0, wkv0, wl, vec)


def _rwkv_vec_table(p, layer):
    mu = p["rwkv_mu"][layer]
    rows = [mu[0:D_C], mu[D_C:2 * D_C], mu[2 * D_C:3 * D_C], p["rwkv_w0"][layer], p["rwkv_a0"][layer],
            p["rwkv_k_k"][layer], p["rwkv_k_a"][layer], p["rwkv_r_k"][layer].reshape(D_C),
            p["rwkv_lnx_g"][layer], p["rwkv_lnx_b"][layer], jnp.pad(mu[3 * D_C:], (0, D_C - D_LORA))]
    rows += [jnp.zeros((D_C,), F32)] * (16 - len(rows))
    return jnp.stack(rows, 0)


def _rwkv_prompt(pc, batch, seq, wl, vec):
    return pl.pallas_call(
        _rwkv_prompt_kernel,
        grid=(batch,),
        in_specs=[pl.BlockSpec((seq, D_C_PROJ), lambda b: (b, 0)),
                  pl.BlockSpec((D_LORA, 3 * D_C), lambda b: (0, 0)),
                  pl.BlockSpec((16, D_C), lambda b: (0, 0))],
        out_specs=[pl.BlockSpec((seq, D_C), lambda b: (b, 0)),
                   pl.BlockSpec((None, N_HEADS_C, HEAD_DIM, HEAD_DIM), lambda b: (b, 0, 0, 0))],
        out_shape=[jax.ShapeDtypeStruct((batch * seq, D_C), F32),
                   jax.ShapeDtypeStruct((batch, N_HEADS_C, HEAD_DIM, HEAD_DIM), F32)],
        scratch_shapes=[pltpu.VMEM((D_C, D_C), F32), pltpu.VMEM((1, D_C_PROJ), F32)],
        compiler_params=_cparams("parallel"),
        name="rwkv_prompt",
    )(pc, wl, vec)


NEAR_ROWS = DILATED_PATTERNS[1][0]
FAR_STRIDE = DILATED_PATTERNS[2][1]


def _sample_key_positions(w_buf, dec_seq):
    far_groups = (w_buf - NEAR_ROWS) // FAR_STRIDE
    far = (np.arange(far_groups)[:, None] * FAR_STRIDE + np.arange(dec_seq)[None, :]).reshape(-1)
    return np.concatenate([far, np.arange(w_buf - NEAR_ROWS, w_buf + dec_seq)])


def _sample_bias_table(w_buf, dec_seq):
    pos = _sample_key_positions(w_buf, dec_seq)
    delta = (w_buf + np.arange(dec_seq))[None, :] - pos[:, None]
    lc = _log_count(delta)
    same = np.eye(N_HEADS_A, dtype=bool)[None, :, :, None]
    tab = np.where(same, lc[:, None, None, :], NEG)
    return tab.reshape(len(pos) * N_HEADS_A, N_HEADS_A * dec_seq).astype(np.float32)


def _attn_sample_kernel(qt_ref, knew_ref, vnew_ref, kfar_ref, knear_ref, vfar_ref, vnear_ref, bias_ref, o_ref):
    def rows(far_ref, near_ref, new_ref):
        far = far_ref[...].reshape(-1, HEAD_DIM)
        near = near_ref[...].reshape(-1, HEAD_DIM)
        return jnp.concatenate([far, near, new_ref[...]], axis=0).astype(BF16)

    k = rows(kfar_ref, knear_ref, knew_ref)
    v = rows(vfar_ref, vnear_ref, vnew_ref)
    cols = qt_ref.shape[1]
    s = jnp.dot(k, qt_ref[...].astype(BF16), preferred_element_type=F32) + bias_ref[...]
    s = s.reshape(-1, N_HEADS_A, cols)
    m = jnp.max(s, axis=0, keepdims=True)
    p = jnp.exp(s - m)
    head = lax.broadcasted_iota(jnp.int32, (1, N_HEADS_A, cols), 1)
    col_head = lax.broadcasted_iota(jnp.int32, (1, N_HEADS_A, cols), 2) // (cols // N_HEADS_A)
    own = head == col_head
    denom = jnp.where(own, jnp.sum(p, axis=0, keepdims=True), 1.0)
    p = jnp.where(own, p / denom, 0.0).reshape(-1, cols).astype(BF16)
    o_ref[...] = _dg(p, v, 0, 0)


def _attn_sample(qkv_s, cache_k, cache_v, layer, batch, dec_seq):
    w_buf = cache_k.shape[2]
    assert w_buf % NEAR_ROWS == 0 and NEAR_ROWS % FAR_STRIDE == 0 and dec_seq <= FAR_STRIDE
    far_groups = (w_buf - NEAR_ROWS) // FAR_STRIDE
    cols = N_HEADS_A * dec_seq
    heads = lambda t: t.reshape(batch, dec_seq, N_HEADS_A, HEAD_DIM)
    q = heads(qkv_s[:, 0:D_A]) * HEAD_DIM ** -0.5
    qt = jnp.transpose(q, (0, 3, 2, 1)).reshape(batch, HEAD_DIM, cols)
    k_new = qkv_s[:, D_A:2 * D_A].reshape(batch, dec_seq * N_HEADS_A, HEAD_DIM)
    v_new = qkv_s[:, 2 * D_A:].reshape(batch, dec_seq * N_HEADS_A, HEAD_DIM)
    bias = jnp.asarray(_sample_bias_table(w_buf, dec_seq))
    strided = lambda c: c.reshape(c.shape[0], batch, w_buf // FAR_STRIDE, FAR_STRIDE, N_HEADS_A, HEAD_DIM)
    far = pl.BlockSpec((None, None, far_groups, dec_seq, N_HEADS_A, HEAD_DIM), lambda b: (layer, b, 0, 0, 0, 0))
    near = pl.BlockSpec((None, None, NEAR_ROWS, N_HEADS_A, HEAD_DIM), lambda b: (layer, b, w_buf // NEAR_ROWS - 1, 0, 0))
    new = pl.BlockSpec((None, dec_seq * N_HEADS_A, HEAD_DIM), lambda b: (b, 0, 0))
    o = pl.pallas_call(
        _attn_sample_kernel,
        grid=(batch,),
        in_specs=[pl.BlockSpec((None, HEAD_DIM, cols), lambda b: (b, 0, 0)), new, new, far, near, far, near,
                  pl.BlockSpec(bias.shape, lambda b: (0, 0))],
        out_specs=pl.BlockSpec((None, cols, HEAD_DIM), lambda b: (b, 0, 0)),
        out_shape=jax.ShapeDtypeStruct((batch, cols, HEAD_DIM), F32),
        compiler_params=_cparams("parallel"),
        name="attn_sample",
    )(qt, k_new, v_new, strided(cache_k), cache_k, strided(cache_v), cache_v, bias)
    o = o.reshape(batch, N_HEADS_A, dec_seq, HEAD_DIM)
    return jnp.transpose(o, (0, 2, 1, 3)).reshape(batch * dec_seq, D_A)


def _sgu_sample_coef(ws, bias, dec_seq):
    assert 8 % dec_seq == 0
    t = np.arange(8) % dec_seq
    out = []
    for k in range(dec_seq):
        w = ws[:, :, t, np.maximum(t - k, 0)] * jnp.asarray(t >= k, F32)
        out.append(w)
    out.append(bias[:, :, t])
    coef = jnp.stack(out, axis=1)
    return jnp.repeat(jnp.swapaxes(coef, 2, 3), HEAD_DIM, axis=3)


def _sgu_sample_kernel(u_ref, v_ref, coef_ref, g_ref, b_ref, o_ref, vn_ref):
    n, width = u_ref.shape
    taps = coef_ref.shape[0] - 1
    u = _gelu(u_ref[...])
    vn = _layer_norm(_gelu(v_ref[...]), g_ref[...], b_ref[...])
    vn_ref[...] = vn
    z = jnp.broadcast_to(coef_ref[taps][None], (n // 8, 8, width))
    for k in range(taps):
        shifted = vn if k == 0 else pltpu.roll(vn, k, axis=0)
        z = z + coef_ref[k][None] * shifted.reshape(n // 8, 8, width)
    o_ref[...] = u * z.reshape(n, width)


def _sgu_sample(uv, first_row, n, coef, g, b, layer):
    blk = first_row // n
    vec = pl.BlockSpec((None, 1, D_B), lambda i: (layer, 0, 0))
    return pl.pallas_call(
        _sgu_sample_kernel,
        grid=(1,),
        in_specs=[pl.BlockSpec((n, D_B), lambda i: (blk, 0)), pl.BlockSpec((n, D_B), lambda i: (blk, 1)),
                  pl.BlockSpec((None,) + coef.shape[1:], lambda i: (layer, 0, 0, 0)), vec, vec],
        out_specs=[pl.BlockSpec((n, D_B), lambda i: (0, 0)), pl.BlockSpec((n, D_B), lambda i: (0, 0))],
        out_shape=[jax.ShapeDtypeStruct((n, D_B), F32), jax.ShapeDtypeStruct((n, D_B), F32)],
        compiler_params=_cparams("arbitrary"),
        name="sgu_sample",
    )(uv, uv, coef, g, b)


def _router_weights(r1, r1b, r2, r2b):
    depth = r1.shape[0]
    w = jnp.concatenate([r1, r2, jnp.zeros((depth, D_MODEL, LANES - N_GROUPS - N_EXPERTS), F32)], axis=-1)
    b = jnp.concatenate([r1b, r2b.reshape(depth, N_EXPERTS), jnp.zeros((depth, LANES - N_GROUPS - N_EXPERTS), F32)], axis=-1)
    return w, b[:, None, :]


def _route(x, wr, br):
    logits = _dot3(x, wr) + br
    lane_i = lax.broadcasted_iota(jnp.int32, logits.shape, 1)
    lane = lane_i.astype(F32)
    far = float(LANES)
    is_g = lane_i < N_GROUPS
    lg = jnp.where(is_g, logits, NEG)
    gmax = jnp.max(lg, axis=-1, keepdims=True)
    grp = jnp.min(jnp.where(lg == gmax, lane, far), axis=-1, keepdims=True)
    gate = 1.0 / jnp.sum(jnp.where(is_g, jnp.exp(lg - gmax), 0.0), axis=-1, keepdims=True)
    lo = N_GROUPS + EXPERTS_PER_GROUP * grp
    le = jnp.where((lane >= lo) & (lane < lo + EXPERTS_PER_GROUP), logits, NEG)
    t1 = jnp.max(le, axis=-1, keepdims=True)
    i1 = jnp.min(jnp.where(le == t1, lane, far), axis=-1, keepdims=True)
    le2 = jnp.where(lane == i1, NEG, le)
    t2 = jnp.max(le2, axis=-1, keepdims=True)
    i2 = jnp.min(jnp.where(le2 == t2, lane, far), axis=-1, keepdims=True)
    e = jnp.exp(t2 - t1)
    w1 = gate / (1.0 + e)
    w2 = gate * e / (1.0 + e)
    return jnp.where(lane_i == 0, i1 - N_GROUPS, jnp.where(lane_i == 1, i2 - N_GROUPS,
           jnp.where(lane_i == 2, w1, jnp.where(lane_i == 3, w2, 0.0))))


def _out_proj_kernel(oa_ref, ob_ref, oc_ref, x_ref, w_ref, g_ref, b_ref, wr_ref, br_ref, x1_ref, route_ref):
    mix = (jnp.dot(oa_ref[...].astype(BF16), w_ref[0:D_A, :], preferred_element_type=F32)
           + jnp.dot(ob_ref[...].astype(BF16), w_ref[D_A:D_A + D_B, :], preferred_element_type=F32)
           + jnp.dot(oc_ref[...].astype(BF16), w_ref[D_A + D_B:, :], preferred_element_type=F32))
    x1 = _layer_norm(DEEPNORM_ALPHA * x_ref[...] + mix, g_ref[...], b_ref[...])
    x1_ref[...] = x1
    route_ref[...] = _route(x1, wr_ref[...], br_ref[...])


def _out_proj(oa, ob, oc, x, w_out_bf16, g, b, wr, br, layer):
    n = x.shape[0]
    row = lambda w: pl.BlockSpec((ROW_TILE, w), lambda i: (i, 0))
    per_layer = lambda *shape: pl.BlockSpec((None,) + shape, lambda i: (layer,) + (0,) * len(shape))
    return pl.pallas_call(
        _out_proj_kernel,
        grid=(n // ROW_TILE,),
        in_specs=[row(D_A), row(D_B), row(D_C), row(D_MODEL), per_layer(D_MODEL, D_MODEL),
                  per_layer(1, D_MODEL), per_layer(1, D_MODEL), per_layer(D_MODEL, LANES), per_layer(1, LANES)],
        out_specs=[row(D_MODEL), row(LANES)],
        out_shape=[jax.ShapeDtypeStruct((n, D_MODEL), F32), jax.ShapeDtypeStruct((n, LANES), F32)],
        compiler_params=_cparams("parallel"),
        name="out_proj_ln_route",
    )(oa, ob, oc, x, w_out_bf16, g, b, wr, br)


def _moe_tiles(n_tokens):
    slots = 2 * n_tokens
    return -(-(slots + N_EXPERTS * (MOE_TILE - 1)) // MOE_TILE)


def _moe_plan(route, n_tokens):
    n_tiles = _moe_tiles(n_tokens)
    e = route[:, 0:2].astype(jnp.int32).reshape(-1)
    cw = route[:, 2:4].reshape(-1)
    onehot = (e[:, None] == jnp.arange(N_EXPERTS, dtype=jnp.int32)[None, :]).astype(jnp.int32)
    csum = jnp.cumsum(onehot, axis=0)
    rank = jnp.sum((csum - onehot) * onehot, axis=1)
    counts = csum[-1]
    padded = (counts + MOE_TILE - 1) // MOE_TILE * MOE_TILE
    ends = jnp.cumsum(padded)
    pos = (ends - padded)[e] + rank
    rows = n_tiles * MOE_TILE
    src = jnp.zeros((rows,), jnp.int32).at[pos].set(jnp.arange(2 * n_tokens, dtype=jnp.int32) // 2)
    cws = jnp.zeros((rows,), F32).at[pos].set(cw)
    tile_expert = jnp.searchsorted(ends, jnp.arange(n_tiles, dtype=jnp.int32) * MOE_TILE, side="right")
    return pos, src, cws, jnp.minimum(tile_expert, N_EXPERTS - 1).astype(jnp.int32)


def _gather_rows_kernel(idx_ref, x_hbm, o_ref, sem):
    rows = o_ref.shape[0]

    def row_copy(j, src_row):
        return pltpu.make_async_copy(x_hbm.at[pl.ds(src_row, 1), :], o_ref.at[pl.ds(j, 1), :], sem)

    def issue(j, _):
        row_copy(j, idx_ref[0, 0, j]).start()
        return 0

    def drain(j, _):
        row_copy(j, 0).wait()
        return 0

    lax.fori_loop(0, rows, issue, 0, unroll=8)
    lax.fori_loop(0, rows, drain, 0, unroll=8)


def _gather_rows(x, idx, tile):
    n_out = idx.shape[0]
    width = x.shape[1]
    return pl.pallas_call(
        _gather_rows_kernel,
        grid=(n_out // tile,),
        in_specs=[pl.BlockSpec((1, 1, tile), lambda i: (i, 0, 0), memory_space=pltpu.SMEM),
                  pl.BlockSpec(memory_space=pl.ANY)],
        out_specs=pl.BlockSpec((tile, width), lambda i: (i, 0)),
        out_shape=jax.ShapeDtypeStruct((n_out, width), x.dtype),
        scratch_shapes=[pltpu.SemaphoreType.DMA],
        compiler_params=_cparams("arbitrary"),
        name="gather_rows",
    )(idx.reshape(n_out // tile, 1, tile), x)


def _expert_kernel(te_ref, x_ref, cw_ref, wg_ref, wu_ref, wd_ref, o_ref):
    x = x_ref[...].astype(BF16)
    g = jnp.dot(x, wg_ref[...].astype(BF16), preferred_element_type=F32)
    u = jnp.dot(x, wu_ref[...].astype(BF16), preferred_element_type=F32)
    h = (g * _sigmoid(g)) * u * cw_ref[...]
    o_ref[...] = jnp.dot(h.astype(BF16), wd_ref[...].astype(BF16), preferred_element_type=F32)


def _experts(xs, cws, tile_expert, w_gate, w_up, w_down, layer):
    rows = xs.shape[0]
    w_gate = w_gate.reshape(DEPTH * N_EXPERTS, D_MODEL, D_EXPERT)
    w_up = w_up.reshape(DEPTH * N_EXPERTS, D_MODEL, D_EXPERT)
    w_down = w_down.reshape(DEPTH * N_EXPERTS, D_EXPERT, D_MODEL)
    expert = lambda i, te: (layer * N_EXPERTS + te[i], 0, 0)
    return pl.pallas_call(
        _expert_kernel,
        grid_spec=pltpu.PrefetchScalarGridSpec(
            num_scalar_prefetch=1,
            grid=(rows // MOE_TILE,),
            in_specs=[pl.BlockSpec((MOE_TILE, D_MODEL), lambda i, te: (i, 0)),
                      pl.BlockSpec((MOE_TILE, 1), lambda i, te: (i, 0)),
                      pl.BlockSpec((None, D_MODEL, D_EXPERT), expert),
                      pl.BlockSpec((None, D_MODEL, D_EXPERT), expert),
                      pl.BlockSpec((None, D_EXPERT, D_MODEL), expert)],
            out_specs=pl.BlockSpec((MOE_TILE, D_MODEL), lambda i, te: (i, 0))),
        out_shape=jax.ShapeDtypeStruct((rows, D_MODEL), F32),
        compiler_params=_cparams("arbitrary"),
        name="experts",
    )(tile_expert, xs, cws.reshape(rows, 1), w_gate, w_up, w_down)


def _combine_ln_kernel(y_ref, x_ref, g_ref, b_ref, o_ref):
    y = y_ref[:, 0:D_MODEL] + y_ref[:, D_MODEL:]
    o_ref[...] = _layer_norm(DEEPNORM_ALPHA * x_ref[...] + y, g_ref[...], b_ref[...])


def _combine_ln(y_slots, x1, g, b, layer):
    n = x1.shape[0]
    vec = pl.BlockSpec((None, 1, D_MODEL), lambda i: (layer, 0, 0))
    return pl.pallas_call(
        _combine_ln_kernel,
        grid=(n // ROW_TILE,),
        in_specs=[pl.BlockSpec((ROW_TILE, 2 * D_MODEL), lambda i: (i, 0)),
                  pl.BlockSpec((ROW_TILE, D_MODEL), lambda i: (i, 0)), vec, vec],
        out_specs=pl.BlockSpec((ROW_TILE, D_MODEL), lambda i: (i, 0)),
        out_shape=jax.ShapeDtypeStruct((n, D_MODEL), F32),
        compiler_params=_cparams("parallel"),
        name="combine_ln",
    )(y_slots.reshape(n, 2 * D_MODEL), x1, g, b)


def _moe(x1, route, p, layer):
    n = x1.shape[0]
    pos, src, cws, tile_expert = _moe_plan(route, n)
    xs = _gather_rows(x1, src, MOE_TILE)
    ys = _experts(xs, cws, tile_expert, p["moe_w_gate"], p["moe_w_up"], p["moe_w_down"], layer)
    y_slots = _gather_rows(ys, pos, ROW_TILE)
    return _combine_ln(y_slots, x1, p["ln2_g3"], p["ln2_b3"], layer)


def kernel(x_prompt, x_sample, cache_win_k, cache_win_v, state_wkv, state_shift, w_in, w_out, sgu_ln_g, sgu_ln_b, sgu_ws, sgu_bias, rwkv_mu, rwkv_w0, rwkv_w2, rwkv_a0, rwkv_a2, rwkv_g2, rwkv_k_k, rwkv_k_a, rwkv_r_k, rwkv_lnx_g, rwkv_lnx_b, ln1_g, ln1_b, ln2_g, ln2_b, moe_router1, moe_router1_b, moe_router2, moe_router2_b, moe_w_gate, moe_w_up, moe_w_down):
    bp, tp, _ = x_prompt.shape
    bs, ts, _ = x_sample.shape
    n_p, n_s = bp * tp, bs * ts
    depth = w_in.shape[0]
    assert depth == DEPTH and tp <= DILATED_PATTERNS[-1][0] and tp % CHUNK == 0
    assert n_p % ROW_TILE == 0 and n_s % ROW_TILE == 0 and n_p % n_s == 0

    per_row = lambda a: a[:, None, :]
    p = dict(rwkv_mu=rwkv_mu, rwkv_w0=rwkv_w0, rwkv_a0=rwkv_a0, rwkv_k_k=rwkv_k_k, rwkv_k_a=rwkv_k_a,
             rwkv_r_k=rwkv_r_k, rwkv_lnx_g=rwkv_lnx_g, rwkv_lnx_b=rwkv_lnx_b,
             moe_w_gate=moe_w_gate, moe_w_up=moe_w_up, moe_w_down=moe_w_down,
             ln2_g3=per_row(ln2_g), ln2_b3=per_row(ln2_b))
    w_in_b = w_in.astype(BF16)
    w_out_b = w_out.astype(BF16)
    lora = _rwkv_lora_weights(rwkv_w2, rwkv_a2, rwkv_g2)
    wr, br = _router_weights(moe_router1, moe_router1_b, moe_router2, moe_router2_b)
    sgu_bias_full = _expand_sgu_bias(sgu_bias)
    sgu_coef = _sgu_sample_coef(sgu_ws, sgu_bias, ts)
    sgu_g, sgu_b = per_row(sgu_ln_g), per_row(sgu_ln_b)
    shift0 = state_shift[:, :, None, :]

    x = jnp.concatenate([x_prompt.reshape(n_p, D_MODEL), x_sample.reshape(n_s, D_MODEL)], axis=0)
    outs = [[] for _ in range(9)]
    for layer in range(depth):
        qkv, uv, pc = _in_proj(x, w_in_b, layer)
        vec = _rwkv_vec_table(p, layer)

        oa_p = _attn_prompt(qkv, bp, tp)
        ob_p = _sgu_prompt(uv, n_p, sgu_ws, sgu_bias_full, sgu_g, sgu_b, layer)
        oc_p, wkv_p = _rwkv_prompt(pc, bp, tp, lora[layer], vec)

        qkv_s = qkv[n_p:]
        pc_s = pc[n_p:].reshape(bs, ts, D_C_PROJ)
        oa_s = _attn_sample(qkv_s, cache_win_k, cache_win_v, layer, bs, ts)
        ob_s, vn_s = _sgu_sample(uv, n_p, n_s, sgu_coef, sgu_g, sgu_b, layer)
        oc_s, wkv_s = _rwkv_sample(jnp.pad(pc_s, ((0, 0), (0, 8 - ts), (0, 0))), shift0, state_wkv,
                                   lora[layer], vec, layer, ts)
        oc_s = oc_s[:, :ts].reshape(n_s, D_C)

        x1, route = _out_proj(jnp.concatenate([oa_p, oa_s], axis=0), jnp.concatenate([ob_p, ob_s], axis=0),
                              jnp.concatenate([oc_p, oc_s], axis=0), x, w_out_b,
                              per_row(ln1_g), per_row(ln1_b), wr, br, layer)
        x = _moe(x1, route, p, layer)

        heads_p = lambda t: t.reshape(bp, tp, N_HEADS_A, HEAD_DIM)
        heads_s = lambda t: t.reshape(bs, ts, N_HEADS_A, HEAD_DIM)
        layer_outs = (heads_p(qkv[:n_p, D_A:2 * D_A]), heads_p(qkv[:n_p, 2 * D_A:]), wkv_p,
                      pc[:n_p].reshape(bp, tp, D_C_PROJ)[:, -1],
                      heads_s(qkv_s[:, D_A:2 * D_A]), heads_s(qkv_s[:, 2 * D_A:]), wkv_s, pc_s[:, -1],
                      vn_s.reshape(bs, ts, D_B))
        for acc, o in zip(outs, layer_outs):
            acc.append(o)

    return (x[:n_p].reshape(bp, tp, D_MODEL), x[n_p:].reshape(bs, ts, D_MODEL)) + tuple(jnp.stack(o, 0) for o in outs)
```

```python
import functools
import math

import numpy as np
import jax
import jax.numpy as jnp
from jax import lax
from jax.experimental import pallas as pl
from jax.experimental.pallas import tpu as pltpu

F32 = jnp.float32
BF16 = jnp.bfloat16

D_MODEL = 1024
HEAD_DIM = 64
N_HEADS_A = 8
N_HEADS_B = 4
N_HEADS_C = 4
D_A = N_HEADS_A * HEAD_DIM
D_B = N_HEADS_B * HEAD_DIM
D_C = N_HEADS_C * HEAD_DIM
DILATED_PATTERNS = ((128, 1), (512, 4), (2048, 16))
CHUNK = 128
LORA_W, LORA_A, LORA_G = 32, 32, 64
D_LORA = LORA_W + LORA_A + LORA_G
D_C_PROJ = 3 * D_C + D_LORA
D_QKV = 3 * D_A
D_UV = 2 * D_B
D_IN = D_QKV + D_UV + D_C_PROJ
N_GROUPS = 4
EXPERTS_PER_GROUP = 8
N_EXPERTS = N_GROUPS * EXPERTS_PER_GROUP
D_EXPERT = 512
DEPTH = 2
DEEPNORM_ALPHA = (2 * DEPTH) ** 0.25
LN_EPS = 1e-5
GN_EPS = 64e-5
DECAY_SCALE = math.exp(-0.5)
L2_EPS = 1e-12
NEG = -1e30

LANES = 128
ROW_TILE = 512
Q_BLOCK = 128
RWKV_CHUNK = 64
MOE_TILE = 256
VMEM_LIMIT = 56 * 1024 * 1024


def _cparams(*sem):
    return pltpu.CompilerParams(dimension_semantics=sem, vmem_limit_bytes=VMEM_LIMIT)


def _dg(a, b, ca=1, cb=0):
    return lax.dot_general(a, b, (((ca,), (cb,)), ((), ())), preferred_element_type=F32)


def _split(x):
    hi = x.astype(BF16)
    lo = (x - hi.astype(F32)).astype(BF16)
    return hi, lo


def _dot3(a, b, ca=1, cb=0):
    ah, al = _split(a)
    bh, bl = _split(b)
    return _dg(ah, bh, ca, cb) + (_dg(ah, bl, ca, cb) + _dg(al, bh, ca, cb))


def _dot_exact_lhs(a_bf16, b, ca=1, cb=0):
    bh, bl = _split(b)
    return _dg(a_bf16, bh, ca, cb) + _dg(a_bf16, bl, ca, cb)


def _dot_exact_rhs(a, b_bf16, ca=1, cb=0):
    ah, al = _split(a)
    return _dg(ah, b_bf16, ca, cb) + _dg(al, b_bf16, ca, cb)


def _sigmoid(x):
    return 1.0 / (1.0 + jnp.exp(-x))


def _gelu(x):
    return 0.5 * x * (1.0 + lax.erf(x * (2.0 ** -0.5)))


def _layer_norm(z, g, b):
    mu = jnp.mean(z, axis=-1, keepdims=True)
    d = z - mu
    var = jnp.mean(d * d, axis=-1, keepdims=True)
    return d * lax.rsqrt(var + LN_EPS) * g + b


def _in_proj_sample_kernel(x_ref, w_ref, qkv_ref, uv_ref, pc_ref):
    x = x_ref[...].astype(BF16)
    qkv_ref[...] = jnp.dot(x, w_ref[:, 0:D_QKV], preferred_element_type=F32)
    uv_ref[...] = jnp.dot(x, w_ref[:, D_QKV:D_QKV + D_UV], preferred_element_type=F32)
    pc_ref[...] = jnp.dot(x, w_ref[:, D_QKV + D_UV:D_IN], preferred_element_type=F32)


def _in_proj_sample(x, first_row, n, w_in_bf16, layer):
    blk = first_row // n
    return pl.pallas_call(
        _in_proj_sample_kernel,
        grid=(1,),
        in_specs=[pl.BlockSpec((n, D_MODEL), lambda i: (blk, 0)),
                  pl.BlockSpec((None, D_MODEL, D_IN), lambda i: (layer, 0, 0))],
        out_specs=[pl.BlockSpec((n, w), lambda i: (0, 0)) for w in (D_QKV, D_UV, D_C_PROJ)],
        out_shape=[jax.ShapeDtypeStruct((n, w), F32) for w in (D_QKV, D_UV, D_C_PROJ)],
        compiler_params=_cparams("arbitrary"),
        name="in_proj_sample",
    )(x, w_in_bf16)


def _in_proj_prompt_kernel(x_ref, w_ref, wkvt_ref, q_ref, kt_ref, vt_ref, uv_ref, pc_ref):
    x = x_ref[...].astype(BF16)
    q_ref[...] = jnp.dot(x, w_ref[:, 0:D_A], preferred_element_type=F32)
    kt_ref[...] = _dg(wkvt_ref[0:D_A, :], x, 1, 1)
    vt_ref[...] = _dg(wkvt_ref[D_A:, :], x, 1, 1)
    uv_ref[...] = jnp.dot(x, w_ref[:, D_QKV:D_QKV + D_UV], preferred_element_type=F32)
    pc_ref[...] = jnp.dot(x, w_ref[:, D_QKV + D_UV:D_IN], preferred_element_type=F32)


def _in_proj_prompt(x, batch, seq, w_in_bf16, w_kvt_bf16, layer):
    n = batch * seq
    per_seq = seq // ROW_TILE
    row = lambda w: pl.BlockSpec((ROW_TILE, w), lambda i: (i, 0))
    tr = pl.BlockSpec((None, D_A, ROW_TILE), lambda i: (i // per_seq, 0, i % per_seq))
    return pl.pallas_call(
        _in_proj_prompt_kernel,
        grid=(n // ROW_TILE,),
        in_specs=[row(D_MODEL), pl.BlockSpec((None, D_MODEL, D_IN), lambda i: (layer, 0, 0)),
                  pl.BlockSpec((None, 2 * D_A, D_MODEL), lambda i: (layer, 0, 0))],
        out_specs=[row(D_A), tr, tr, row(D_UV), row(D_C_PROJ)],
        out_shape=[jax.ShapeDtypeStruct((n, D_A), F32), jax.ShapeDtypeStruct((batch, D_A, seq), F32),
                   jax.ShapeDtypeStruct((batch, D_A, seq), F32), jax.ShapeDtypeStruct((n, D_UV), F32),
                   jax.ShapeDtypeStruct((n, D_C_PROJ), F32)],
        compiler_params=_cparams("parallel"),
        name="in_proj_prompt",
    )(x, w_in_bf16, w_kvt_bf16)


def _pattern_count(delta):
    c = np.zeros(delta.shape, np.float64)
    for window, dilation in DILATED_PATTERNS:
        c += (delta >= 0) & (delta <= window) & (delta % dilation == 0)
    return c


def _log_count(delta):
    c = _pattern_count(delta)
    return np.where(c > 0, np.log(np.maximum(c, 1.0)), NEG).astype(np.float32)


def _prompt_bias_table(seq):
    i = np.arange(Q_BLOCK)[:, None]
    n = np.arange(seq)[None, :]
    tab = _log_count(seq - Q_BLOCK + i - n)
    return np.concatenate([tab, tab], axis=0)


def _attn_prompt_kernel(q_ref, kt_ref, vt_ref, bias_ref, o_ref, kb_ref, vb_ref, *, nb):
    seq = nb * Q_BLOCK
    kb_ref[...] = kt_ref[...].astype(BF16)
    vb_ref[...] = vt_ref[...].astype(BF16)
    first = lax.broadcasted_iota(jnp.int32, (Q_BLOCK, LANES), 1) < HEAD_DIM
    scale = HEAD_DIM ** -0.5
    for qb in range(nb):
        keys = (qb + 1) * Q_BLOCK
        q = q_ref[qb * Q_BLOCK:(qb + 1) * Q_BLOCK, :] * scale
        q2 = jnp.concatenate([jnp.where(first, q, 0.0), jnp.where(first, 0.0, q)], axis=0).astype(BF16)
        s = jnp.dot(q2, kb_ref[:, 0:keys], preferred_element_type=F32) + bias_ref[:, seq - keys:seq]
        p = jnp.exp(s - jnp.max(s, axis=-1, keepdims=True))
        denom = jnp.sum(p, axis=-1, keepdims=True)
        o = _dg(p.astype(BF16), vb_ref[:, 0:keys], 1, 1) / denom
        o_ref[qb * Q_BLOCK:(qb + 1) * Q_BLOCK, :] = jnp.where(first, o[:Q_BLOCK], o[Q_BLOCK:])


def _attn_prompt(q, kt, vt, batch, seq):
    nb = seq // Q_BLOCK
    pairs = D_A // LANES
    bias = jnp.asarray(_prompt_bias_table(seq))
    tr = pl.BlockSpec((None, LANES, seq), lambda b, p: (b, p, 0))
    return pl.pallas_call(
        functools.partial(_attn_prompt_kernel, nb=nb),
        grid=(batch, pairs),
        in_specs=[pl.BlockSpec((seq, LANES), lambda b, p: (b, p)), tr, tr,
                  pl.BlockSpec((2 * Q_BLOCK, seq), lambda b, p: (0, 0))],
        out_specs=pl.BlockSpec((seq, LANES), lambda b, p: (b, p)),
        out_shape=jax.ShapeDtypeStruct((batch * seq, D_A), F32),
        scratch_shapes=[pltpu.VMEM((LANES, seq), BF16), pltpu.VMEM((LANES, seq), BF16)],
        compiler_params=_cparams("parallel", "parallel"),
        name="attn_prompt",
    )(q, kt, vt, bias)


def _sgu_prompt_kernel(u_ref, v_ref, ws_ref, bias_ref, g_ref, b_ref, o_ref):
    u = _gelu(u_ref[...])
    vn = _layer_norm(_gelu(v_ref[...]), g_ref[...], b_ref[...]).astype(BF16)
    r = lax.broadcasted_iota(jnp.int32, (CHUNK, CHUNK), 0)
    c = lax.broadcasted_iota(jnp.int32, (CHUNK, CHUNK), 1)
    head = lax.broadcasted_iota(jnp.int32, (CHUNK, D_B), 1) // HEAD_DIM
    z = bias_ref[...]
    for h in range(N_HEADS_B):
        wm = jnp.where(r >= c, ws_ref[h], 0.0).astype(BF16)
        z = z + jnp.where(head == h, jnp.dot(wm, vn, preferred_element_type=F32), 0.0)
    o_ref[...] = u * z


def _expand_sgu_bias(bias):
    return jnp.repeat(jnp.swapaxes(bias, 1, 2), HEAD_DIM, axis=2)


def _sgu_prompt(uv, n, ws, bias_full, g, b, layer):
    vec = pl.BlockSpec((None, 1, D_B), lambda i: (layer, 0, 0))
    return pl.pallas_call(
        _sgu_prompt_kernel,
        grid=(n // CHUNK,),
        in_specs=[pl.BlockSpec((CHUNK, D_B), lambda i: (i, 0)), pl.BlockSpec((CHUNK, D_B), lambda i: (i, 1)),
                  pl.BlockSpec((None, N_HEADS_B, CHUNK, CHUNK), lambda i: (layer, 0, 0, 0)),
                  pl.BlockSpec((None, CHUNK, D_B), lambda i: (layer, 0, 0)), vec, vec],
        out_specs=pl.BlockSpec((CHUNK, D_B), lambda i: (i, 0)),
        out_shape=jax.ShapeDtypeStruct((n, D_B), F32),
        compiler_params=_cparams("parallel"),
        name="sgu_prompt",
    )(uv, uv, ws, bias_full, g, b)


def _rwkv_lora_weights(w2, a2, g2):
    depth = w2.shape[0]
    w = jnp.zeros((depth, D_LORA, 3 * D_C), F32)
    w = w.at[:, 0:LORA_W, 0:D_C].set(w2)
    w = w.at[:, LORA_W:LORA_W + LORA_A, D_C:2 * D_C].set(a2)
    return w.at[:, LORA_W + LORA_A:, 2 * D_C:].set(g2)


def _rwkv_features(xs, wl, vec):
    r = xs[:, 0:D_C]
    k = xs[:, D_C:2 * D_C]
    v = xs[:, 2 * D_C:3 * D_C]
    tail = xs[:, 3 * D_C:]
    lane = lax.broadcasted_iota(jnp.int32, tail.shape, 1)
    feats = jnp.where(lane < LORA_W, jnp.tanh(tail), jnp.where(lane < LORA_W + LORA_A, tail, _sigmoid(tail)))
    lora = _dot3(feats, wl)
    logw = -DECAY_SCALE * _sigmoid(vec["w0"] + lora[:, 0:D_C])
    a = _sigmoid(vec["a0"] + lora[:, D_C:2 * D_C])
    gate = lora[:, 2 * D_C:]
    kk = k * vec["k_k"]
    ri = lax.broadcasted_iota(jnp.int32, (D_C, D_C), 0) // HEAD_DIM
    ci = lax.broadcasted_iota(jnp.int32, (D_C, D_C), 1) // HEAD_DIM
    head_ones = jnp.where(ri == ci, 1.0, 0.0).astype(BF16)
    kk = kk / jnp.maximum(jnp.sqrt(_dot_exact_rhs(kk * kk, head_ones)), L2_EPS)
    k = k * (1.0 + (a - 1.0) * vec["k_a"])
    bonus = _dot_exact_rhs(r * k * vec["r_k"], head_ones) * v
    return r, k, v, logw, kk, kk * a, gate, bonus


def _rwkv_chunk_fn(C, n_valid, wl_ref, vec_ref):
    H = N_HEADS_C
    names = ("mu_r", "mu_k", "mu_v", "w0", "a0", "k_k", "k_a", "r_k", "lnx_g", "lnx_b")
    vec = {n: vec_ref[i:i + 1, :] for i, n in enumerate(names)}
    mu = jnp.concatenate([vec["mu_r"], vec["mu_k"], vec["mu_v"], vec_ref[len(names):len(names) + 1, 0:D_LORA]], axis=1)

    row = lax.broadcasted_iota(jnp.int32, (C, D_C_PROJ), 0)
    is_token = lax.broadcasted_iota(jnp.int32, (C, D_C), 0) < n_valid
    srow = lax.broadcasted_iota(jnp.int32, (H * C, H * C), 0)
    scol = lax.broadcasted_iota(jnp.int32, (H * C, H * C), 1)
    strict = (srow % C) > (scol % C)
    incl = (srow % C) >= (scol % C)
    eye = jnp.where(srow == scol, 1.0, 0.0)
    stack_head = lax.broadcasted_iota(jnp.int32, (H * C, D_C), 0) // C
    stack_lane_head = lax.broadcasted_iota(jnp.int32, (H * C, D_C), 1) // HEAD_DIM
    own = stack_head == stack_lane_head
    tr = lax.broadcasted_iota(jnp.int32, (C, C), 0)
    tc = lax.broadcasted_iota(jnp.int32, (C, C), 1)
    cum = jnp.where(tr >= tc, 1.0, 0.0).astype(BF16)

    def stack(x):
        return jnp.where(own, jnp.concatenate([x] * H, axis=0), 0.0)

    def unstack(xs):
        return (xs[0:C] + xs[C:2 * C]) + (xs[2 * C:3 * C] + xs[3 * C:4 * C])

    def chunk(pc, prev_row, s0):
        prev = jnp.where(row == 0, prev_row, pltpu.roll(pc, 1, axis=0))
        xs = pc + (prev - pc) * mu
        r, k, v, logw, kk, b, gate, bonus = _rwkv_features(xs, wl_ref[...], vec)
        if n_valid < C:
            logw = jnp.where(is_token, logw, 0.0)
            kk = jnp.where(is_token, kk, 0.0)
            b = jnp.where(is_token, b, 0.0)
            k = jnp.where(is_token, k, 0.0)

        lc = _dot_exact_lhs(cum, logw)
        lc_end = lc[C - 1:C, :]
        w_inv = jnp.exp(-lc)
        a_t = stack(-kk * jnp.exp(lc - logw))
        r_t = stack(r * jnp.exp(lc))
        b_t = stack(b * w_inv)
        k_t = stack(k * w_inv)
        to_end = jnp.exp(lc_end - lc)
        b_e = stack(b * to_end)
        k_e = stack(k * to_end)
        v_s = stack(v)

        n = H * C
        ar = jnp.concatenate([a_t, r_t], axis=0).astype(BF16)
        gram = _dg(ar, jnp.concatenate([b_t, k_t], axis=0).astype(BF16), 1, 1)
        ab = jnp.where(strict, gram[:n, :n], 0.0)
        ak = jnp.where(strict, gram[:n, n:], 0.0).astype(BF16)
        rb = jnp.where(incl, gram[n:, :n], 0.0).astype(BF16)
        rk = jnp.where(incl, gram[n:, n:], 0.0).astype(BF16)

        inv = eye + ab
        power = ab.astype(BF16)
        for step in range(int(math.log2(C)) - 1):
            power = _dg(power, power)
            inv = inv + _dg(power.astype(BF16), inv.astype(BF16))
            power = power.astype(BF16)

        from_s0 = _dg(ar, s0.astype(BF16), 1, 1)
        v_b = v_s.astype(BF16)
        u_b = _dg(inv.astype(BF16), (from_s0[:n] + _dg(ak, v_b)).astype(BF16)).astype(BF16)
        y = unstack(from_s0[n:] + _dg(rb, u_b) + _dg(rk, v_b))
        s_end = s0 * jnp.exp(lc_end) + _dg(u_b, b_e.astype(BF16), 0, 0) + _dg(v_b, k_e.astype(BF16), 0, 0)

        ys = stack(y)
        mean = jnp.sum(ys, axis=-1, keepdims=True) * (1.0 / HEAD_DIM)
        d = jnp.where(own, ys - mean, 0.0)
        var = jnp.sum(d * d, axis=-1, keepdims=True) * (1.0 / HEAD_DIM)
        yn = unstack(d * lax.rsqrt(var + GN_EPS)) * vec["lnx_g"] + vec["lnx_b"]
        return (yn + bonus) * gate, s_end

    return chunk


def _store_head_states(s_out_ref, s):
    for h in range(N_HEADS_C):
        s_out_ref[h] = s[h * HEAD_DIM:(h + 1) * HEAD_DIM, h * HEAD_DIM:(h + 1) * HEAD_DIM]


def _rwkv_prompt_kernel(pc_ref, wl_ref, vec_ref, o_ref, s_out_ref, s_ref, prev_ref):
    C = RWKV_CHUNK
    group, span, _ = pc_ref.shape
    chunk = _rwkv_chunk_fn(C, C, wl_ref, vec_ref)

    @pl.when(pl.program_id(1) == 0)
    def _():
        s_ref[...] = jnp.zeros_like(s_ref)
        prev_ref[...] = jnp.zeros_like(prev_ref)

    def step(c, _):
        t0 = pl.multiple_of(c * C, C)
        for g in range(group):
            pc = pc_ref[g, pl.ds(t0, C), :]
            out, s_end = chunk(pc, prev_ref[g], s_ref[g])
            prev_ref[g] = pc[C - 1:C, :]
            s_ref[g] = s_end
            o_ref[g, pl.ds(t0, C), :] = out
        return 0

    lax.fori_loop(0, span // C, step, 0)

    @pl.when(pl.program_id(1) == pl.num_programs(1) - 1)
    def _():
        for g in range(group):
            _store_head_states(s_out_ref.at[g], s_ref[g])


def _rwkv_sample_kernel(pc_ref, shift_ref, s_in_ref, wl_ref, vec_ref, o_ref, s_out_ref, *, n_valid):
    C = pc_ref.shape[0]
    zero = jnp.zeros((HEAD_DIM, HEAD_DIM), F32)
    s0 = jnp.concatenate(
        [jnp.concatenate([s_in_ref[h] if g == h else zero for g in range(N_HEADS_C)], axis=1)
         for h in range(N_HEADS_C)], axis=0)
    out, s_end = _rwkv_chunk_fn(C, n_valid, wl_ref, vec_ref)(pc_ref[...], shift_ref[...], s0)
    o_ref[...] = out
    _store_head_states(s_out_ref, s_end)


def _rwkv_sample(pc_pad, shift0, wkv0, wl, vec, layer, n_valid):
    batch, rows, _ = pc_pad.shape
    state = pl.BlockSpec((None, None, N_HEADS_C, HEAD_DIM, HEAD_DIM), lambda b: (layer, b, 0, 0, 0))
    return pl.pallas_call(
        functools.partial(_rwkv_sample_kernel, n_valid=n_valid),
        grid=(batch,),
        in_specs=[pl.BlockSpec((None, rows, D_C_PROJ), lambda b: (b, 0, 0)),
                  pl.BlockSpec((None, None, 1, D_C_PROJ), lambda b: (layer, b, 0, 0)),
                  state,
                  pl.BlockSpec((D_LORA, 3 * D_C), lambda b: (0, 0)),
                  pl.BlockSpec((16, D_C), lambda b: (0, 0))],
        out_specs=[pl.BlockSpec((None, rows, D_C), lambda b: (b, 0, 0)),
                   pl.BlockSpec((None, N_HEADS_C, HEAD_DIM, HEAD_DIM), lambda b: (b, 0, 0, 0))],
        out_shape=[jax.ShapeDtypeStruct((batch, rows, D_C), F32),
                   jax.ShapeDtypeStruct((batch, N_HEADS_C, HEAD_DIM, HEAD_DIM), F32)],
        compiler_params=_cparams("parallel"),
        name="rwkv_sample",
    )(pc_pad, shift0, wkv0, wl, vec)


def _rwkv_vec_table(p, layer):
    mu = p["rwkv_mu"][layer]
    rows = [mu[0:D_C], mu[D_C:2 * D_C], mu[2 * D_C:3 * D_C], p["rwkv_w0"][layer], p["rwkv_a0"][layer],
            p["rwkv_k_k"][layer], p["rwkv_k_a"][layer], p["rwkv_r_k"][layer].reshape(D_C),
            p["rwkv_lnx_g"][layer], p["rwkv_lnx_b"][layer], jnp.pad(mu[3 * D_C:], (0, D_C - D_LORA))]
    rows += [jnp.zeros((D_C,), F32)] * (16 - len(rows))
    return jnp.stack(rows, 0)


RWKV_GROUP = 8
RWKV_SPAN = 256


def _rwkv_prompt(pc, batch, seq, wl, vec):
    group = math.gcd(batch, RWKV_GROUP)
    span = math.gcd(seq, RWKV_SPAN)
    o, s = pl.pallas_call(
        _rwkv_prompt_kernel,
        grid=(batch // group, seq // span),
        in_specs=[pl.BlockSpec((group, span, D_C_PROJ), lambda b, t: (b, t, 0)),
                  pl.BlockSpec((D_LORA, 3 * D_C), lambda b, t: (0, 0)),
                  pl.BlockSpec((16, D_C), lambda b, t: (0, 0))],
        out_specs=[pl.BlockSpec((group, span, D_C), lambda b, t: (b, t, 0)),
                   pl.BlockSpec((group, N_HEADS_C, HEAD_DIM, HEAD_DIM), lambda b, t: (b, 0, 0, 0))],
        out_shape=[jax.ShapeDtypeStruct((batch, seq, D_C), F32),
                   jax.ShapeDtypeStruct((batch, N_HEADS_C, HEAD_DIM, HEAD_DIM), F32)],
        scratch_shapes=[pltpu.VMEM((group, D_C, D_C), F32), pltpu.VMEM((group, 1, D_C_PROJ), F32)],
        compiler_params=_cparams("parallel", "arbitrary"),
        name="rwkv_prompt",
    )(pc[:batch * seq].reshape(batch, seq, D_C_PROJ), wl, vec)
    return o.reshape(batch * seq, D_C), s


NEW_PAD = 8


def _sample_bias_tables(w_buf, dec_seq):
    q_pos = w_buf + np.repeat(np.arange(dec_seq), N_HEADS_A)[:, None]
    window = _log_count(q_pos - np.arange(w_buf)[None, :])
    new = _log_count(q_pos - (w_buf + np.arange(NEW_PAD))[None, :])
    new[:, dec_seq:] = NEG
    return window, new


def _attn_sample_kernel(q_ref, knew_ref, vnew_ref, kt_ref, vt_ref, bias_w_ref, bias_n_ref, o_ref):
    rows = q_ref.shape[0]
    dec_seq = rows // N_HEADS_A
    q = q_ref[...].astype(BF16)
    s_w = jnp.dot(q, kt_ref[...].astype(BF16), preferred_element_type=F32) + bias_w_ref[...]
    s_n = _dg(q, knew_ref[...].astype(BF16), 1, 1) + bias_n_ref[...]
    m = jnp.maximum(jnp.max(s_w, axis=-1, keepdims=True), jnp.max(s_n, axis=-1, keepdims=True))
    p_w = jnp.exp(s_w - m)
    p_n = jnp.exp(s_n - m)
    denom = jnp.sum(p_w, axis=-1, keepdims=True) + jnp.sum(p_n, axis=-1, keepdims=True)
    o = (_dg(p_w.astype(BF16), vt_ref[...].astype(BF16), 1, 1)
         + jnp.dot(p_n.astype(BF16), vnew_ref[...].astype(BF16), preferred_element_type=F32)) / denom
    head = lax.broadcasted_iota(jnp.int32, (N_HEADS_A, D_A), 0)
    col_head = lax.broadcasted_iota(jnp.int32, (N_HEADS_A, D_A), 1) // HEAD_DIM
    for i in range(dec_seq):
        tile = o[i * N_HEADS_A:(i + 1) * N_HEADS_A, :]
        o_ref[i:i + 1, :] = jnp.sum(jnp.where(head == col_head, tile, 0.0), axis=0, keepdims=True)


def _attn_sample(qkv_s, cache_kt, cache_vt, layer, batch, dec_seq):
    w_buf = cache_kt.shape[3]
    assert dec_seq <= NEW_PAD
    rows = dec_seq * N_HEADS_A
    q = qkv_s[:, 0:D_A].reshape(batch, dec_seq, N_HEADS_A, 1, HEAD_DIM) * HEAD_DIM ** -0.5
    q_bd = (q * jnp.eye(N_HEADS_A, dtype=F32)[None, None, :, :, None]).reshape(batch, rows, D_A)
    pad_new = lambda t: jnp.pad(t.reshape(batch, dec_seq, D_A), ((0, 0), (0, NEW_PAD - dec_seq), (0, 0)))
    bias_w, bias_n = (jnp.asarray(t) for t in _sample_bias_tables(w_buf, dec_seq))
    cache = pl.BlockSpec((None, None, D_A, w_buf), lambda b: (layer, b, 0, 0))
    new = pl.BlockSpec((None, NEW_PAD, D_A), lambda b: (b, 0, 0))
    o = pl.pallas_call(
        _attn_sample_kernel,
        grid=(batch,),
        in_specs=[pl.BlockSpec((None, rows, D_A), lambda b: (b, 0, 0)), new, new, cache, cache,
                  pl.BlockSpec(bias_w.shape, lambda b: (0, 0)), pl.BlockSpec(bias_n.shape, lambda b: (0, 0))],
        out_specs=pl.BlockSpec((None, dec_seq, D_A), lambda b: (b, 0, 0)),
        out_shape=jax.ShapeDtypeStruct((batch, dec_seq, D_A), F32),
        compiler_params=_cparams("parallel"),
        name="attn_sample",
    )(q_bd, pad_new(qkv_s[:, D_A:2 * D_A]), pad_new(qkv_s[:, 2 * D_A:]), cache_kt, cache_vt, bias_w, bias_n)
    return o.reshape(batch * dec_seq, D_A)


def _window_transposed(cache):
    depth, batch, w_buf = cache.shape[:3]
    return jnp.transpose(cache, (0, 1, 3, 4, 2)).reshape(depth, batch, D_A, w_buf)


def _sgu_sample_coef(ws, bias, dec_seq):
    assert 8 % dec_seq == 0
    t = np.arange(8) % dec_seq
    out = []
    for k in range(dec_seq):
        w = ws[:, :, t, np.maximum(t - k, 0)] * jnp.asarray(t >= k, F32)
        out.append(w)
    out.append(bias[:, :, t])
    coef = jnp.stack(out, axis=1)
    return jnp.repeat(jnp.swapaxes(coef, 2, 3), HEAD_DIM, axis=3)


def _sgu_sample_kernel(u_ref, v_ref, coef_ref, g_ref, b_ref, o_ref, vn_ref):
    n, width = u_ref.shape
    taps = coef_ref.shape[0] - 1
    u = _gelu(u_ref[...])
    vn = _layer_norm(_gelu(v_ref[...]), g_ref[...], b_ref[...])
    vn_ref[...] = vn
    z = jnp.broadcast_to(coef_ref[taps][None], (n // 8, 8, width))
    for k in range(taps):
        shifted = vn if k == 0 else pltpu.roll(vn, k, axis=0)
        z = z + coef_ref[k][None] * shifted.reshape(n // 8, 8, width)
    o_ref[...] = u * z.reshape(n, width)


def _sgu_sample(uv, first_row, n, coef, g, b, layer):
    blk = first_row // n
    vec = pl.BlockSpec((None, 1, D_B), lambda i: (layer, 0, 0))
    return pl.pallas_call(
        _sgu_sample_kernel,
        grid=(1,),
        in_specs=[pl.BlockSpec((n, D_B), lambda i: (blk, 0)), pl.BlockSpec((n, D_B), lambda i: (blk, 1)),
                  pl.BlockSpec((None,) + coef.shape[1:], lambda i: (layer, 0, 0, 0)), vec, vec],
        out_specs=[pl.BlockSpec((n, D_B), lambda i: (0, 0)), pl.BlockSpec((n, D_B), lambda i: (0, 0))],
        out_shape=[jax.ShapeDtypeStruct((n, D_B), F32), jax.ShapeDtypeStruct((n, D_B), F32)],
        compiler_params=_cparams("arbitrary"),
        name="sgu_sample",
    )(uv, uv, coef, g, b)


def _router_weights(r1, r1b, r2, r2b):
    depth = r1.shape[0]
    w = jnp.concatenate([r1, r2, jnp.zeros((depth, D_MODEL, LANES - N_GROUPS - N_EXPERTS), F32)], axis=-1)
    b = jnp.concatenate([r1b, r2b.reshape(depth, N_EXPERTS), jnp.zeros((depth, LANES - N_GROUPS - N_EXPERTS), F32)], axis=-1)
    return w, b[:, None, :]


def _route(x, wr, br):
    logits = _dot3(x, wr) + br
    lane_i = lax.broadcasted_iota(jnp.int32, logits.shape, 1)
    lane = lane_i.astype(F32)
    far = float(LANES)
    is_g = lane_i < N_GROUPS
    lg = jnp.where(is_g, logits, NEG)
    gmax = jnp.max(lg, axis=-1, keepdims=True)
    grp = jnp.min(jnp.where(lg == gmax, lane, far), axis=-1, keepdims=True)
    gate = 1.0 / jnp.sum(jnp.where(is_g, jnp.exp(lg - gmax), 0.0), axis=-1, keepdims=True)
    lo = N_GROUPS + EXPERTS_PER_GROUP * grp
    le = jnp.where((lane >= lo) & (lane < lo + EXPERTS_PER_GROUP), logits, NEG)
    t1 = jnp.max(le, axis=-1, keepdims=True)
    i1 = jnp.min(jnp.where(le == t1, lane, far), axis=-1, keepdims=True)
    le2 = jnp.where(lane == i1, NEG, le)
    t2 = jnp.max(le2, axis=-1, keepdims=True)
    i2 = jnp.min(jnp.where(le2 == t2, lane, far), axis=-1, keepdims=True)
    e = jnp.exp(t2 - t1)
    w1 = gate / (1.0 + e)
    w2 = gate * e / (1.0 + e)
    return jnp.where(lane_i == 0, i1 - N_GROUPS, jnp.where(lane_i == 1, i2 - N_GROUPS,
           jnp.where(lane_i == 2, w1, jnp.where(lane_i == 3, w2, 0.0))))


def _out_proj_kernel(oa_ref, ob_ref, oc_ref, x_ref, w_ref, g_ref, b_ref, wr_ref, br_ref, x1_ref, route_ref):
    mix = (jnp.dot(oa_ref[...].astype(BF16), w_ref[0:D_A, :], preferred_element_type=F32)
           + jnp.dot(ob_ref[...].astype(BF16), w_ref[D_A:D_A + D_B, :], preferred_element_type=F32)
           + jnp.dot(oc_ref[...].astype(BF16), w_ref[D_A + D_B:, :], preferred_element_type=F32))
    x1 = _layer_norm(DEEPNORM_ALPHA * x_ref[...] + mix, g_ref[...], b_ref[...])
    x1_ref[...] = x1
    route_ref[...] = _route(x1, wr_ref[...], br_ref[...])


def _out_proj(oa, ob, oc, x, w_out_bf16, g, b, wr, br, layer):
    n = x.shape[0]
    row = lambda w: pl.BlockSpec((ROW_TILE, w), lambda i: (i, 0))
    per_layer = lambda *shape: pl.BlockSpec((None,) + shape, lambda i: (layer,) + (0,) * len(shape))
    return pl.pallas_call(
        _out_proj_kernel,
        grid=(n // ROW_TILE,),
        in_specs=[row(D_A), row(D_B), row(D_C), row(D_MODEL), per_layer(D_MODEL, D_MODEL),
                  per_layer(1, D_MODEL), per_layer(1, D_MODEL), per_layer(D_MODEL, LANES), per_layer(1, LANES)],
        out_specs=[row(D_MODEL), row(LANES)],
        out_shape=[jax.ShapeDtypeStruct((n, D_MODEL), F32), jax.ShapeDtypeStruct((n, LANES), F32)],
        compiler_params=_cparams("parallel"),
        name="out_proj_ln_route",
    )(oa, ob, oc, x, w_out_bf16, g, b, wr, br)


def _moe_tiles(n_tokens):
    slots = 2 * n_tokens
    return -(-(slots + N_EXPERTS * (MOE_TILE - 1)) // MOE_TILE)


PLAN_BLOCK = 128


def _moe_plan(route, n_tokens):
    n_tiles = _moe_tiles(n_tokens)
    e = route[:, 0:2].astype(jnp.int32).reshape(-1, PLAN_BLOCK)
    onehot = (e[:, :, None] == jnp.arange(N_EXPERTS, dtype=jnp.int32)).astype(F32)
    tri = jnp.tril(jnp.ones((PLAN_BLOCK, PLAN_BLOCK), F32))
    within = jnp.einsum("ts,bse->bte", tri, onehot)
    block_total = within[:, -1, :]
    block_start = jnp.cumsum(block_total, axis=0) - block_total
    counts = (block_start[-1] + block_total[-1]).astype(jnp.int32)
    padded = (counts + MOE_TILE - 1) // MOE_TILE * MOE_TILE
    ends = jnp.cumsum(padded)
    starts = (ends - padded).astype(F32)
    pos = jnp.sum(onehot * (starts + block_start[:, None, :] + within - onehot), axis=-1)
    tile_start = jnp.arange(n_tiles, dtype=jnp.int32) * MOE_TILE
    tile_expert = jnp.sum((ends[None, :] <= tile_start[:, None]).astype(jnp.int32), axis=1)
    info = jnp.concatenate([ends, ends[-1:] // MOE_TILE]).astype(jnp.int32)
    return pos.astype(jnp.int32).reshape(-1), info, jnp.minimum(tile_expert, N_EXPERTS - 1)


def _dispatch_kernel(info_ref, pos_ref, x_ref, xs_hbm, zero_ref, sem, zero_sem, *, n_tiles):
    tokens = x_ref.shape[0]

    def zero_tile(row0):
        return pltpu.make_async_copy(zero_ref, xs_hbm.at[pl.ds(pl.multiple_of(row0, MOE_TILE), MOE_TILE), :], zero_sem)

    def for_each_pad_tile(act):
        for e in range(N_EXPERTS):
            start = info_ref[e - 1] if e else 0

            @pl.when(info_ref[e] > start)
            def _():
                act(zero_tile(info_ref[e] - MOE_TILE))

        def tail(t, _):
            act(zero_tile(t * MOE_TILE))
            return 0

        lax.fori_loop(info_ref[N_EXPERTS], n_tiles, tail, 0)

    @pl.when(pl.program_id(0) == 0)
    def _():
        zero_ref[...] = jnp.zeros_like(zero_ref)
        for_each_pad_tile(lambda cp: cp.start())
        for_each_pad_tile(lambda cp: cp.wait())

    def row_copy(j, dst_row):
        return pltpu.make_async_copy(x_ref.at[pl.ds(j // 2, 1), :], xs_hbm.at[pl.ds(dst_row, 1), :], sem)

    def issue(j, _):
        row_copy(j, pos_ref[0, 0, j]).start()
        return 0

    def drain(j, _):
        row_copy(j, 0).wait()
        return 0

    lax.fori_loop(0, 2 * tokens, issue, 0, unroll=8)
    lax.fori_loop(0, 2 * tokens, drain, 0, unroll=8)


def _dispatch(x1, pos, info):
    n = x1.shape[0]
    n_tiles = _moe_tiles(n)
    steps = n // ROW_TILE
    return pl.pallas_call(
        functools.partial(_dispatch_kernel, n_tiles=n_tiles),
        grid_spec=pltpu.PrefetchScalarGridSpec(
            num_scalar_prefetch=1,
            grid=(steps,),
            in_specs=[pl.BlockSpec((1, 1, 2 * ROW_TILE), lambda i, info: (i, 0, 0), memory_space=pltpu.SMEM),
                      pl.BlockSpec((ROW_TILE, D_MODEL), lambda i, info: (i, 0))],
            out_specs=pl.BlockSpec(memory_space=pl.ANY),
            scratch_shapes=[pltpu.VMEM((MOE_TILE, D_MODEL), F32), pltpu.SemaphoreType.DMA, pltpu.SemaphoreType.DMA]),
        out_shape=jax.ShapeDtypeStruct((n_tiles * MOE_TILE, D_MODEL), F32),
        compiler_params=_cparams("arbitrary"),
        name="moe_dispatch",
    )(info, pos.reshape(steps, 1, 2 * ROW_TILE), x1)


def _expert_kernel(te_ref, info_ref, x_ref, wg_ref, wu_ref, wd_ref, o_ref):
    used = pl.program_id(0) < info_ref[N_EXPERTS]

    @pl.when(used)
    def _():
        x = x_ref[...].astype(BF16)
        g = jnp.dot(x, wg_ref[...].astype(BF16), preferred_element_type=F32)
        u = jnp.dot(x, wu_ref[...].astype(BF16), preferred_element_type=F32)
        h = (g * _sigmoid(g)) * u
        o_ref[...] = jnp.dot(h.astype(BF16), wd_ref[...].astype(BF16), preferred_element_type=F32)

    @pl.when(jnp.logical_not(used))
    def _():
        o_ref[...] = jnp.zeros_like(o_ref)


def _experts(xs, tile_expert, info, w_gate, w_up, w_down, layer):
    rows = xs.shape[0]
    w_gate = w_gate.reshape(DEPTH * N_EXPERTS, D_MODEL, D_EXPERT)
    w_up = w_up.reshape(DEPTH * N_EXPERTS, D_MODEL, D_EXPERT)
    w_down = w_down.reshape(DEPTH * N_EXPERTS, D_EXPERT, D_MODEL)
    expert = lambda i, te, info: (layer * N_EXPERTS + te[i], 0, 0)
    return pl.pallas_call(
        _expert_kernel,
        grid_spec=pltpu.PrefetchScalarGridSpec(
            num_scalar_prefetch=2,
            grid=(rows // MOE_TILE,),
            in_specs=[pl.BlockSpec((MOE_TILE, D_MODEL), lambda i, te, info: (i, 0)),
                      pl.BlockSpec((None, D_MODEL, D_EXPERT), expert),
                      pl.BlockSpec((None, D_MODEL, D_EXPERT), expert),
                      pl.BlockSpec((None, D_EXPERT, D_MODEL), expert)],
            out_specs=pl.BlockSpec((MOE_TILE, D_MODEL), lambda i, te, info: (i, 0))),
        out_shape=jax.ShapeDtypeStruct((rows, D_MODEL), F32),
        compiler_params=_cparams("arbitrary"),
        name="experts",
    )(tile_expert, info, xs, w_gate, w_up, w_down)


COMBINE_TILE = 256


def _combine_ln_kernel(pos_ref, pos_next_ref, ys_hbm, x_ref, route_ref, g_ref, b_ref, o_ref, buf_ref, sem):
    tokens = x_ref.shape[0]
    i = pl.program_id(0)
    slot = i % 2

    def row_copy(idx_ref, j, into):
        dst = (j % 2) * tokens + j // 2
        return pltpu.make_async_copy(ys_hbm.at[pl.ds(idx_ref[0, 0, j], 1), :],
                                     buf_ref.at[into, pl.ds(dst, 1), :], sem.at[into])

    def issue(idx_ref, into):
        def body(j, _):
            row_copy(idx_ref, j, into).start()
            return 0
        lax.fori_loop(0, 2 * tokens, body, 0, unroll=8)

    @pl.when(i == 0)
    def _():
        issue(pos_ref, 0)

    @pl.when(i + 1 < pl.num_programs(0))
    def _():
        issue(pos_next_ref, 1 - slot)

    def drain(j, _):
        row_copy(pos_ref, j, slot).wait()
        return 0

    lax.fori_loop(0, 2 * tokens, drain, 0, unroll=8)
    route = route_ref[...]
    y = route[:, 2:3] * buf_ref[slot, 0:tokens, :] + route[:, 3:4] * buf_ref[slot, tokens:, :]
    o_ref[...] = _layer_norm(DEEPNORM_ALPHA * x_ref[...] + y, g_ref[...], b_ref[...])


def _combine_ln(ys, pos, x1, route, g, b, layer):
    n = x1.shape[0]
    steps = n // COMBINE_TILE
    vec = pl.BlockSpec((None, 1, D_MODEL), lambda i: (layer, 0, 0))
    idx = lambda f: pl.BlockSpec((1, 1, 2 * COMBINE_TILE), lambda i: (f(i), 0, 0), memory_space=pltpu.SMEM)
    pos = pos.reshape(steps, 1, 2 * COMBINE_TILE)
    return pl.pallas_call(
        _combine_ln_kernel,
        grid=(steps,),
        in_specs=[idx(lambda i: i), idx(lambda i: jnp.minimum(i + 1, steps - 1)),
                  pl.BlockSpec(memory_space=pl.ANY),
                  pl.BlockSpec((COMBINE_TILE, D_MODEL), lambda i: (i, 0)),
                  pl.BlockSpec((COMBINE_TILE, LANES), lambda i: (i, 0)), vec, vec],
        out_specs=pl.BlockSpec((COMBINE_TILE, D_MODEL), lambda i: (i, 0)),
        out_shape=jax.ShapeDtypeStruct((n, D_MODEL), F32),
        scratch_shapes=[pltpu.VMEM((2, 2 * COMBINE_TILE, D_MODEL), F32), pltpu.SemaphoreType.DMA((2,))],
        compiler_params=_cparams("arbitrary"),
        name="moe_combine_ln",
    )(pos, pos, ys, x1, route, g, b)


def _moe(x1, route, p, layer):
    pos, info, tile_expert = _moe_plan(route, x1.shape[0])
    xs = _dispatch(x1, pos, info)
    ys = _experts(xs, tile_expert, info, p["moe_w_gate"], p["moe_w_up"], p["moe_w_down"], layer)
    return _combine_ln(ys, pos, x1, route, p["ln2_g3"], p["ln2_b3"], layer)


def kernel(x_prompt, x_sample, cache_win_k, cache_win_v, state_wkv, state_shift, w_in, w_out, sgu_ln_g, sgu_ln_b, sgu_ws, sgu_bias, rwkv_mu, rwkv_w0, rwkv_w2, rwkv_a0, rwkv_a2, rwkv_g2, rwkv_k_k, rwkv_k_a, rwkv_r_k, rwkv_lnx_g, rwkv_lnx_b, ln1_g, ln1_b, ln2_g, ln2_b, moe_router1, moe_router1_b, moe_router2, moe_router2_b, moe_w_gate, moe_w_up, moe_w_down):
    bp, tp, _ = x_prompt.shape
    bs, ts, _ = x_sample.shape
    n_p, n_s = bp * tp, bs * ts
    depth = w_in.shape[0]
    assert depth == DEPTH and tp <= DILATED_PATTERNS[-1][0] and tp % CHUNK == 0
    assert n_p % ROW_TILE == 0 and n_s % ROW_TILE == 0 and n_p % n_s == 0

    per_row = lambda a: a[:, None, :]
    p = dict(rwkv_mu=rwkv_mu, rwkv_w0=rwkv_w0, rwkv_a0=rwkv_a0, rwkv_k_k=rwkv_k_k, rwkv_k_a=rwkv_k_a,
             rwkv_r_k=rwkv_r_k, rwkv_lnx_g=rwkv_lnx_g, rwkv_lnx_b=rwkv_lnx_b,
             moe_w_gate=moe_w_gate, moe_w_up=moe_w_up, moe_w_down=moe_w_down,
             ln2_g3=per_row(ln2_g), ln2_b3=per_row(ln2_b))
    w_in_b = w_in.astype(BF16)
    w_kvt_b = jnp.swapaxes(w_in[:, :, D_A:3 * D_A], 1, 2).astype(BF16)
    w_out_b = w_out.astype(BF16)
    cache_kt, cache_vt = _window_transposed(cache_win_k), _window_transposed(cache_win_v)
    lora = _rwkv_lora_weights(rwkv_w2, rwkv_a2, rwkv_g2)
    wr, br = _router_weights(moe_router1, moe_router1_b, moe_router2, moe_router2_b)
    sgu_bias_full = _expand_sgu_bias(sgu_bias)
    sgu_coef = _sgu_sample_coef(sgu_ws, sgu_bias, ts)
    sgu_g, sgu_b = per_row(sgu_ln_g), per_row(sgu_ln_b)
    shift0 = state_shift[:, :, None, :]

    x = jnp.concatenate([x_prompt.reshape(n_p, D_MODEL), x_sample.reshape(n_s, D_MODEL)], axis=0)
    outs = [[] for _ in range(9)]
    for layer in range(depth):
        q_p, kt_p, vt_p, uv_p, pc_p = _in_proj_prompt(x, bp, tp, w_in_b, w_kvt_b, layer)
        qkv_s, uv_s, pc_s = _in_proj_sample(x, n_p, n_s, w_in_b, layer)
        vec = _rwkv_vec_table(p, layer)

        oa_p = _attn_prompt(q_p, kt_p, vt_p, bp, tp)
        ob_p = _sgu_prompt(uv_p, n_p, sgu_ws, sgu_bias_full, sgu_g, sgu_b, layer)
        oc_p, wkv_p = _rwkv_prompt(pc_p, bp, tp, lora[layer], vec)

        pc_s = pc_s.reshape(bs, ts, D_C_PROJ)
        oa_s = _attn_sample(qkv_s, cache_kt, cache_vt, layer, bs, ts)
        ob_s, vn_s = _sgu_sample(uv_s, 0, n_s, sgu_coef, sgu_g, sgu_b, layer)
        oc_s, wkv_s = _rwkv_sample(jnp.pad(pc_s, ((0, 0), (0, 8 - ts), (0, 0))), shift0, state_wkv,
                                   lora[layer], vec, layer, ts)
        oc_s = oc_s[:, :ts].reshape(n_s, D_C)

        x1, route = _out_proj(jnp.concatenate([oa_p, oa_s], axis=0), jnp.concatenate([ob_p, ob_s], axis=0),
                              jnp.concatenate([oc_p, oc_s], axis=0), x, w_out_b,
                              per_row(ln1_g), per_row(ln1_b), wr, br, layer)
        x = _moe(x1, route, p, layer)

        heads_p = lambda t: jnp.transpose(t.reshape(bp, N_HEADS_A, HEAD_DIM, tp), (0, 3, 1, 2))
        heads_s = lambda t: t.reshape(bs, ts, N_HEADS_A, HEAD_DIM)
        layer_outs = (heads_p(kt_p), heads_p(vt_p), wkv_p,
                      pc_p.reshape(bp, tp, D_C_PROJ)[:, -1],
                      heads_s(qkv_s[:, D_A:2 * D_A]), heads_s(qkv_s[:, 2 * D_A:]), wkv_s, pc_s[:, -1],
                      vn_s.reshape(bs, ts, D_B))
        for acc, o in zip(outs, layer_outs):
            acc.append(o)

    return (x[:n_p].reshape(bp, tp, D_MODEL), x[n_p:].reshape(bs, ts, D_MODEL)) + tuple(jnp.stack(o, 0) for o in outs)
```

```python
import functools
import math

import numpy as np
import jax
import jax.numpy as jnp
from jax import lax
from jax.experimental import pallas as pl
from jax.experimental.pallas import tpu as pltpu

F32 = jnp.float32
BF16 = jnp.bfloat16

D_MODEL = 1024
HEAD_DIM = 64
N_HEADS_A = 8
N_HEADS_B = 4
N_HEADS_C = 4
D_A = N_HEADS_A * HEAD_DIM
D_B = N_HEADS_B * HEAD_DIM
D_C = N_HEADS_C * HEAD_DIM
DILATED_PATTERNS = ((128, 1), (512, 4), (2048, 16))
CHUNK = 128
LORA_W, LORA_A, LORA_G = 32, 32, 64
D_LORA = LORA_W + LORA_A + LORA_G
D_C_PROJ = 3 * D_C + D_LORA
D_QKV = 3 * D_A
D_UV = 2 * D_B
D_IN = D_QKV + D_UV + D_C_PROJ
N_GROUPS = 4
EXPERTS_PER_GROUP = 8
N_EXPERTS = N_GROUPS * EXPERTS_PER_GROUP
D_EXPERT = 512
DEPTH = 2
DEEPNORM_ALPHA = (2 * DEPTH) ** 0.25
LN_EPS = 1e-5
GN_EPS = 64e-5
DECAY_SCALE = math.exp(-0.5)
L2_EPS = 1e-12
NEG = -1e30

LANES = 128
ROW_TILE = 512
Q_BLOCK = 128
RWKV_CHUNK = 64
MOE_TILE = 256
VMEM_LIMIT = 56 * 1024 * 1024


def _cparams(*sem):
    return pltpu.CompilerParams(dimension_semantics=sem, vmem_limit_bytes=VMEM_LIMIT)


def _dg(a, b, ca=1, cb=0):
    return lax.dot_general(a, b, (((ca,), (cb,)), ((), ())), preferred_element_type=F32)


def _split(x):
    hi = x.astype(BF16)
    lo = (x - hi.astype(F32)).astype(BF16)
    return hi, lo


def _dot3(a, b, ca=1, cb=0):
    ah, al = _split(a)
    bh, bl = _split(b)
    return _dg(ah, bh, ca, cb) + (_dg(ah, bl, ca, cb) + _dg(al, bh, ca, cb))


def _dot_exact_lhs(a_bf16, b, ca=1, cb=0):
    bh, bl = _split(b)
    return _dg(a_bf16, bh, ca, cb) + _dg(a_bf16, bl, ca, cb)


def _dot_exact_rhs(a, b_bf16, ca=1, cb=0):
    ah, al = _split(a)
    return _dg(ah, b_bf16, ca, cb) + _dg(al, b_bf16, ca, cb)


def _sigmoid(x):
    return 1.0 / (1.0 + jnp.exp(-x))


def _gelu(x):
    return 0.5 * x * (1.0 + lax.erf(x * (2.0 ** -0.5)))


def _layer_norm(z, g, b):
    mu = jnp.mean(z, axis=-1, keepdims=True)
    d = z - mu
    var = jnp.mean(d * d, axis=-1, keepdims=True)
    return d * lax.rsqrt(var + LN_EPS) * g + b


def _in_proj_sample_kernel(x_ref, w_ref, qkv_ref, uv_ref, pc_ref):
    x = x_ref[...].astype(BF16)
    qkv_ref[...] = jnp.dot(x, w_ref[:, 0:D_QKV], preferred_element_type=F32)
    uv_ref[...] = jnp.dot(x, w_ref[:, D_QKV:D_QKV + D_UV], preferred_element_type=F32)
    pc_ref[...] = jnp.dot(x, w_ref[:, D_QKV + D_UV:D_IN], preferred_element_type=F32)


def _in_proj_sample(x, first_row, n, w_in_bf16, layer):
    blk = first_row // n
    return pl.pallas_call(
        _in_proj_sample_kernel,
        grid=(1,),
        in_specs=[pl.BlockSpec((n, D_MODEL), lambda i: (blk, 0)),
                  pl.BlockSpec((None, D_MODEL, D_IN), lambda i: (layer, 0, 0))],
        out_specs=[pl.BlockSpec((n, w), lambda i: (0, 0)) for w in (D_QKV, D_UV, D_C_PROJ)],
        out_shape=[jax.ShapeDtypeStruct((n, w), F32) for w in (D_QKV, D_UV, D_C_PROJ)],
        compiler_params=_cparams("arbitrary"),
        name="in_proj_sample",
    )(x, w_in_bf16)


def _in_proj_prompt_kernel(x_ref, w_ref, wkvt_ref, q_ref, kt_ref, vt_ref, uv_ref, pc_ref):
    x = x_ref[...].astype(BF16)
    q_ref[...] = jnp.dot(x, w_ref[:, 0:D_A], preferred_element_type=F32)
    kt_ref[...] = _dg(wkvt_ref[0:D_A, :], x, 1, 1)
    vt_ref[...] = _dg(wkvt_ref[D_A:, :], x, 1, 1)
    uv_ref[...] = jnp.dot(x, w_ref[:, D_QKV:D_QKV + D_UV], preferred_element_type=F32)
    pc_ref[...] = jnp.dot(x, w_ref[:, D_QKV + D_UV:D_IN], preferred_element_type=F32)


def _in_proj_prompt(x, batch, seq, w_in_bf16, w_kvt_bf16, layer):
    n = batch * seq
    per_seq = seq // ROW_TILE
    row = lambda w: pl.BlockSpec((ROW_TILE, w), lambda i: (i, 0))
    tr = pl.BlockSpec((None, D_A, ROW_TILE), lambda i: (i // per_seq, 0, i % per_seq))
    return pl.pallas_call(
        _in_proj_prompt_kernel,
        grid=(n // ROW_TILE,),
        in_specs=[row(D_MODEL), pl.BlockSpec((None, D_MODEL, D_IN), lambda i: (layer, 0, 0)),
                  pl.BlockSpec((None, 2 * D_A, D_MODEL), lambda i: (layer, 0, 0))],
        out_specs=[row(D_A), tr, tr, row(D_UV), row(D_C_PROJ)],
        out_shape=[jax.ShapeDtypeStruct((n, D_A), F32), jax.ShapeDtypeStruct((batch, D_A, seq), F32),
                   jax.ShapeDtypeStruct((batch, D_A, seq), F32), jax.ShapeDtypeStruct((n, D_UV), F32),
                   jax.ShapeDtypeStruct((n, D_C_PROJ), F32)],
        compiler_params=_cparams("parallel"),
        name="in_proj_prompt",
    )(x, w_in_bf16, w_kvt_bf16)


def _pattern_count(delta):
    c = np.zeros(delta.shape, np.float64)
    for window, dilation in DILATED_PATTERNS:
        c += (delta >= 0) & (delta <= window) & (delta % dilation == 0)
    return c


def _log_count(delta):
    c = _pattern_count(delta)
    return np.where(c > 0, np.log(np.maximum(c, 1.0)), NEG).astype(np.float32)


def _prompt_bias_table(seq):
    i = np.arange(Q_BLOCK)[:, None]
    n = np.arange(seq)[None, :]
    tab = _log_count(seq - Q_BLOCK + i - n)
    return np.concatenate([tab, tab], axis=0)


def _attn_prompt_kernel(q_ref, kt_ref, vt_ref, bias_ref, o_ref, kb_ref, vb_ref, *, nb):
    seq = nb * Q_BLOCK
    kb_ref[...] = kt_ref[...].astype(BF16)
    vb_ref[...] = vt_ref[...].astype(BF16)
    first = lax.broadcasted_iota(jnp.int32, (Q_BLOCK, LANES), 1) < HEAD_DIM
    scale = HEAD_DIM ** -0.5
    for qb in range(nb):
        keys = (qb + 1) * Q_BLOCK
        q = q_ref[qb * Q_BLOCK:(qb + 1) * Q_BLOCK, :] * scale
        q2 = jnp.concatenate([jnp.where(first, q, 0.0), jnp.where(first, 0.0, q)], axis=0).astype(BF16)
        s = jnp.dot(q2, kb_ref[:, 0:keys], preferred_element_type=F32) + bias_ref[:, seq - keys:seq]
        p = jnp.exp(s - jnp.max(s, axis=-1, keepdims=True))
        denom = jnp.sum(p, axis=-1, keepdims=True)
        o = _dg(p.astype(BF16), vb_ref[:, 0:keys], 1, 1) / denom
        o_ref[qb * Q_BLOCK:(qb + 1) * Q_BLOCK, :] = jnp.where(first, o[:Q_BLOCK], o[Q_BLOCK:])


def _attn_prompt(q, kt, vt, batch, seq):
    nb = seq // Q_BLOCK
    pairs = D_A // LANES
    bias = jnp.asarray(_prompt_bias_table(seq))
    tr = pl.BlockSpec((None, LANES, seq), lambda b, p: (b, p, 0))
    return pl.pallas_call(
        functools.partial(_attn_prompt_kernel, nb=nb),
        grid=(batch, pairs),
        in_specs=[pl.BlockSpec((seq, LANES), lambda b, p: (b, p)), tr, tr,
                  pl.BlockSpec((2 * Q_BLOCK, seq), lambda b, p: (0, 0))],
        out_specs=pl.BlockSpec((seq, LANES), lambda b, p: (b, p)),
        out_shape=jax.ShapeDtypeStruct((batch * seq, D_A), F32),
        scratch_shapes=[pltpu.VMEM((LANES, seq), BF16), pltpu.VMEM((LANES, seq), BF16)],
        compiler_params=_cparams("parallel", "parallel"),
        name="attn_prompt",
    )(q, kt, vt, bias)


def _sgu_prompt_kernel(u_ref, v_ref, ws_ref, bias_ref, g_ref, b_ref, o_ref):
    u = _gelu(u_ref[...])
    vn = _layer_norm(_gelu(v_ref[...]), g_ref[...], b_ref[...]).astype(BF16)
    r = lax.broadcasted_iota(jnp.int32, (CHUNK, CHUNK), 0)
    c = lax.broadcasted_iota(jnp.int32, (CHUNK, CHUNK), 1)
    head = lax.broadcasted_iota(jnp.int32, (CHUNK, D_B), 1) // HEAD_DIM
    z = bias_ref[...]
    for h in range(N_HEADS_B):
        wm = jnp.where(r >= c, ws_ref[h], 0.0).astype(BF16)
        z = z + jnp.where(head == h, jnp.dot(wm, vn, preferred_element_type=F32), 0.0)
    o_ref[...] = u * z


def _expand_sgu_bias(bias):
    return jnp.repeat(jnp.swapaxes(bias, 1, 2), HEAD_DIM, axis=2)


def _sgu_prompt(uv, n, ws, bias_full, g, b, layer):
    vec = pl.BlockSpec((None, 1, D_B), lambda i: (layer, 0, 0))
    return pl.pallas_call(
        _sgu_prompt_kernel,
        grid=(n // CHUNK,),
        in_specs=[pl.BlockSpec((CHUNK, D_B), lambda i: (i, 0)), pl.BlockSpec((CHUNK, D_B), lambda i: (i, 1)),
                  pl.BlockSpec((None, N_HEADS_B, CHUNK, CHUNK), lambda i: (layer, 0, 0, 0)),
                  pl.BlockSpec((None, CHUNK, D_B), lambda i: (layer, 0, 0)), vec, vec],
        out_specs=pl.BlockSpec((CHUNK, D_B), lambda i: (i, 0)),
        out_shape=jax.ShapeDtypeStruct((n, D_B), F32),
        compiler_params=_cparams("parallel"),
        name="sgu_prompt",
    )(uv, uv, ws, bias_full, g, b)


def _rwkv_lora_weights(w2, a2, g2):
    depth = w2.shape[0]
    w = jnp.zeros((depth, D_LORA, 3 * D_C), F32)
    w = w.at[:, 0:LORA_W, 0:D_C].set(w2)
    w = w.at[:, LORA_W:LORA_W + LORA_A, D_C:2 * D_C].set(a2)
    return w.at[:, LORA_W + LORA_A:, 2 * D_C:].set(g2)


def _rwkv_features(xs, wl, vec):
    r = xs[:, 0:D_C]
    k = xs[:, D_C:2 * D_C]
    v = xs[:, 2 * D_C:3 * D_C]
    tail = xs[:, 3 * D_C:]
    lane = lax.broadcasted_iota(jnp.int32, tail.shape, 1)
    feats = jnp.where(lane < LORA_W, jnp.tanh(tail), jnp.where(lane < LORA_W + LORA_A, tail, _sigmoid(tail)))
    lora = _dot3(feats, wl)
    logw = -DECAY_SCALE * _sigmoid(vec["w0"] + lora[:, 0:D_C])
    a = _sigmoid(vec["a0"] + lora[:, D_C:2 * D_C])
    gate = lora[:, 2 * D_C:]
    kk = k * vec["k_k"]
    ri = lax.broadcasted_iota(jnp.int32, (D_C, D_C), 0) // HEAD_DIM
    ci = lax.broadcasted_iota(jnp.int32, (D_C, D_C), 1) // HEAD_DIM
    head_ones = jnp.where(ri == ci, 1.0, 0.0).astype(BF16)
    kk = kk / jnp.maximum(jnp.sqrt(_dot_exact_rhs(kk * kk, head_ones)), L2_EPS)
    k = k * (1.0 + (a - 1.0) * vec["k_a"])
    bonus = _dot_exact_rhs(r * k * vec["r_k"], head_ones) * v
    return r, k, v, logw, kk, kk * a, gate, bonus


def _rwkv_chunk_fn(C, n_valid, wl_ref, vec_ref):
    H = N_HEADS_C
    names = ("mu_r", "mu_k", "mu_v", "w0", "a0", "k_k", "k_a", "r_k", "lnx_g", "lnx_b")
    vec = {n: vec_ref[i:i + 1, :] for i, n in enumerate(names)}
    mu = jnp.concatenate([vec["mu_r"], vec["mu_k"], vec["mu_v"], vec_ref[len(names):len(names) + 1, 0:D_LORA]], axis=1)

    row = lax.broadcasted_iota(jnp.int32, (C, D_C_PROJ), 0)
    is_token = lax.broadcasted_iota(jnp.int32, (C, D_C), 0) < n_valid
    srow = lax.broadcasted_iota(jnp.int32, (H * C, H * C), 0)
    scol = lax.broadcasted_iota(jnp.int32, (H * C, H * C), 1)
    strict = (srow % C) > (scol % C)
    incl = (srow % C) >= (scol % C)
    eye = jnp.where(srow == scol, 1.0, 0.0)
    stack_head = lax.broadcasted_iota(jnp.int32, (H * C, D_C), 0) // C
    stack_lane_head = lax.broadcasted_iota(jnp.int32, (H * C, D_C), 1) // HEAD_DIM
    own = stack_head == stack_lane_head
    tr = lax.broadcasted_iota(jnp.int32, (C, C), 0)
    tc = lax.broadcasted_iota(jnp.int32, (C, C), 1)
    cum = jnp.where(tr >= tc, 1.0, 0.0).astype(BF16)

    def stack(x):
        return jnp.where(own, jnp.concatenate([x] * H, axis=0), 0.0)

    def unstack(xs):
        return (xs[0:C] + xs[C:2 * C]) + (xs[2 * C:3 * C] + xs[3 * C:4 * C])

    def chunk(pc, prev_row, s0):
        prev = jnp.where(row == 0, prev_row, pltpu.roll(pc, 1, axis=0))
        xs = pc + (prev - pc) * mu
        r, k, v, logw, kk, b, gate, bonus = _rwkv_features(xs, wl_ref[...], vec)
        if n_valid < C:
            logw = jnp.where(is_token, logw, 0.0)
            kk = jnp.where(is_token, kk, 0.0)
            b = jnp.where(is_token, b, 0.0)
            k = jnp.where(is_token, k, 0.0)

        lc = _dot_exact_lhs(cum, logw)
        lc_end = lc[C - 1:C, :]
        w_inv = jnp.exp(-lc)
        a_t = stack(-kk * jnp.exp(lc - logw))
        r_t = stack(r * jnp.exp(lc))
        b_t = stack(b * w_inv)
        k_t = stack(k * w_inv)
        to_end = jnp.exp(lc_end - lc)
        b_e = stack(b * to_end)
        k_e = stack(k * to_end)
        v_s = stack(v)

        n = H * C
        ar = jnp.concatenate([a_t, r_t], axis=0).astype(BF16)
        gram = _dg(ar, jnp.concatenate([b_t, k_t], axis=0).astype(BF16), 1, 1)
        ab = jnp.where(strict, gram[:n, :n], 0.0)
        ak = jnp.where(strict, gram[:n, n:], 0.0).astype(BF16)
        rb = jnp.where(incl, gram[n:, :n], 0.0).astype(BF16)
        rk = jnp.where(incl, gram[n:, n:], 0.0).astype(BF16)

        inv = eye + ab
        power = ab.astype(BF16)
        for step in range(int(math.log2(C)) - 1):
            power = _dg(power, power)
            inv = inv + _dg(power.astype(BF16), inv.astype(BF16))
            power = power.astype(BF16)

        from_s0 = _dg(ar, s0.astype(BF16), 1, 1)
        v_b = v_s.astype(BF16)
        u_b = _dg(inv.astype(BF16), (from_s0[:n] + _dg(ak, v_b)).astype(BF16)).astype(BF16)
        y = unstack(from_s0[n:] + _dg(rb, u_b) + _dg(rk, v_b))
        s_end = s0 * jnp.exp(lc_end) + _dg(u_b, b_e.astype(BF16), 0, 0) + _dg(v_b, k_e.astype(BF16), 0, 0)

        ys = stack(y)
        mean = jnp.sum(ys, axis=-1, keepdims=True) * (1.0 / HEAD_DIM)
        d = jnp.where(own, ys - mean, 0.0)
        var = jnp.sum(d * d, axis=-1, keepdims=True) * (1.0 / HEAD_DIM)
        yn = unstack(d * lax.rsqrt(var + GN_EPS)) * vec["lnx_g"] + vec["lnx_b"]
        return (yn + bonus) * gate, s_end

    return chunk


def _store_head_states(s_out_ref, s):
    for h in range(N_HEADS_C):
        s_out_ref[h] = s[h * HEAD_DIM:(h + 1) * HEAD_DIM, h * HEAD_DIM:(h + 1) * HEAD_DIM]


def _rwkv_prompt_kernel(pc_ref, wl_ref, vec_ref, o_ref, s_out_ref, s_ref, prev_ref):
    C = RWKV_CHUNK
    group, span, _ = pc_ref.shape
    chunk = _rwkv_chunk_fn(C, C, wl_ref, vec_ref)

    @pl.when(pl.program_id(1) == 0)
    def _():
        s_ref[...] = jnp.zeros_like(s_ref)
        prev_ref[...] = jnp.zeros_like(prev_ref)

    def step(c, _):
        t0 = pl.multiple_of(c * C, C)
        for g in range(group):
            pc = pc_ref[g, pl.ds(t0, C), :]
            out, s_end = chunk(pc, prev_ref[g], s_ref[g])
            prev_ref[g] = pc[C - 1:C, :]
            s_ref[g] = s_end
            o_ref[g, pl.ds(t0, C), :] = out
        return 0

    lax.fori_loop(0, span // C, step, 0)

    @pl.when(pl.program_id(1) == pl.num_programs(1) - 1)
    def _():
        for g in range(group):
            _store_head_states(s_out_ref.at[g], s_ref[g])


def _rwkv_sample_kernel(pc_ref, shift_ref, s_in_ref, wl_ref, vec_ref, o_ref, s_out_ref, *, n_valid):
    group, C, _ = pc_ref.shape
    chunk = _rwkv_chunk_fn(C, n_valid, wl_ref, vec_ref)
    zero = jnp.zeros((HEAD_DIM, HEAD_DIM), F32)
    for e in range(group):
        s0 = jnp.concatenate(
            [jnp.concatenate([s_in_ref[e, h] if g == h else zero for g in range(N_HEADS_C)], axis=1)
             for h in range(N_HEADS_C)], axis=0)
        out, s_end = chunk(pc_ref[e], shift_ref[e], s0)
        o_ref[e] = out
        _store_head_states(s_out_ref.at[e], s_end)


RWKV_SAMPLE_GROUP = 4


def _rwkv_sample(pc_pad, shift0, wkv0, wl, vec, layer, n_valid):
    batch, rows, _ = pc_pad.shape
    group = math.gcd(batch, RWKV_SAMPLE_GROUP)
    state = pl.BlockSpec((None, group, N_HEADS_C, HEAD_DIM, HEAD_DIM), lambda b: (layer, b, 0, 0, 0))
    return pl.pallas_call(
        functools.partial(_rwkv_sample_kernel, n_valid=n_valid),
        grid=(batch // group,),
        in_specs=[pl.BlockSpec((group, rows, D_C_PROJ), lambda b: (b, 0, 0)),
                  pl.BlockSpec((None, group, 1, D_C_PROJ), lambda b: (layer, b, 0, 0)),
                  state,
                  pl.BlockSpec((D_LORA, 3 * D_C), lambda b: (0, 0)),
                  pl.BlockSpec((16, D_C), lambda b: (0, 0))],
        out_specs=[pl.BlockSpec((group, rows, D_C), lambda b: (b, 0, 0)),
                   pl.BlockSpec((group, N_HEADS_C, HEAD_DIM, HEAD_DIM), lambda b: (b, 0, 0, 0))],
        out_shape=[jax.ShapeDtypeStruct((batch, rows, D_C), F32),
                   jax.ShapeDtypeStruct((batch, N_HEADS_C, HEAD_DIM, HEAD_DIM), F32)],
        compiler_params=_cparams("parallel"),
        name="rwkv_sample",
    )(pc_pad, shift0, wkv0, wl, vec)


def _rwkv_vec_table(p, layer):
    mu = p["rwkv_mu"][layer]
    rows = [mu[0:D_C], mu[D_C:2 * D_C], mu[2 * D_C:3 * D_C], p["rwkv_w0"][layer], p["rwkv_a0"][layer],
            p["rwkv_k_k"][layer], p["rwkv_k_a"][layer], p["rwkv_r_k"][layer].reshape(D_C),
            p["rwkv_lnx_g"][layer], p["rwkv_lnx_b"][layer], jnp.pad(mu[3 * D_C:], (0, D_C - D_LORA))]
    rows += [jnp.zeros((D_C,), F32)] * (16 - len(rows))
    return jnp.stack(rows, 0)


RWKV_GROUP = 8
RWKV_SPAN = 256


def _rwkv_prompt(pc, batch, seq, wl, vec):
    group = math.gcd(batch, RWKV_GROUP)
    span = math.gcd(seq, RWKV_SPAN)
    o, s = pl.pallas_call(
        _rwkv_prompt_kernel,
        grid=(batch // group, seq // span),
        in_specs=[pl.BlockSpec((group, span, D_C_PROJ), lambda b, t: (b, t, 0)),
                  pl.BlockSpec((D_LORA, 3 * D_C), lambda b, t: (0, 0)),
                  pl.BlockSpec((16, D_C), lambda b, t: (0, 0))],
        out_specs=[pl.BlockSpec((group, span, D_C), lambda b, t: (b, t, 0)),
                   pl.BlockSpec((group, N_HEADS_C, HEAD_DIM, HEAD_DIM), lambda b, t: (b, 0, 0, 0))],
        out_shape=[jax.ShapeDtypeStruct((batch, seq, D_C), F32),
                   jax.ShapeDtypeStruct((batch, N_HEADS_C, HEAD_DIM, HEAD_DIM), F32)],
        scratch_shapes=[pltpu.VMEM((group, D_C, D_C), F32), pltpu.VMEM((group, 1, D_C_PROJ), F32)],
        compiler_params=_cparams("parallel", "arbitrary"),
        name="rwkv_prompt",
    )(pc[:batch * seq].reshape(batch, seq, D_C_PROJ), wl, vec)
    return o.reshape(batch * seq, D_C), s


NEW_PAD = 8


def _sample_bias_tables(w_buf, dec_seq):
    q_pos = w_buf + np.repeat(np.arange(dec_seq), N_HEADS_A)[:, None]
    window = _log_count(q_pos - np.arange(w_buf)[None, :])
    new = _log_count(q_pos - (w_buf + np.arange(NEW_PAD))[None, :])
    new[:, dec_seq:] = NEG
    return window, new


def _attn_sample_kernel(q_ref, knew_ref, vnew_ref, kt_ref, vt_ref, bias_w_ref, bias_n_ref, o_ref):
    rows = q_ref.shape[0]
    dec_seq = rows // N_HEADS_A
    q = q_ref[...].astype(BF16)
    s_w = jnp.dot(q, kt_ref[...].astype(BF16), preferred_element_type=F32) + bias_w_ref[...]
    s_n = _dg(q, knew_ref[...].astype(BF16), 1, 1) + bias_n_ref[...]
    m = jnp.maximum(jnp.max(s_w, axis=-1, keepdims=True), jnp.max(s_n, axis=-1, keepdims=True))
    p_w = jnp.exp(s_w - m)
    p_n = jnp.exp(s_n - m)
    denom = jnp.sum(p_w, axis=-1, keepdims=True) + jnp.sum(p_n, axis=-1, keepdims=True)
    o = (_dg(p_w.astype(BF16), vt_ref[...].astype(BF16), 1, 1)
         + jnp.dot(p_n.astype(BF16), vnew_ref[...].astype(BF16), preferred_element_type=F32)) / denom
    head = lax.broadcasted_iota(jnp.int32, (N_HEADS_A, D_A), 0)
    col_head = lax.broadcasted_iota(jnp.int32, (N_HEADS_A, D_A), 1) // HEAD_DIM
    for i in range(dec_seq):
        tile = o[i * N_HEADS_A:(i + 1) * N_HEADS_A, :]
        o_ref[i:i + 1, :] = jnp.sum(jnp.where(head == col_head, tile, 0.0), axis=0, keepdims=True)


def _attn_sample(qkv_s, cache_kt, cache_vt, layer, batch, dec_seq):
    w_buf = cache_kt.shape[3]
    assert dec_seq <= NEW_PAD
    rows = dec_seq * N_HEADS_A
    q = qkv_s[:, 0:D_A].reshape(batch, dec_seq, N_HEADS_A, 1, HEAD_DIM) * HEAD_DIM ** -0.5
    q_bd = (q * jnp.eye(N_HEADS_A, dtype=F32)[None, None, :, :, None]).reshape(batch, rows, D_A)
    pad_new = lambda t: jnp.pad(t.reshape(batch, dec_seq, D_A), ((0, 0), (0, NEW_PAD - dec_seq), (0, 0)))
    bias_w, bias_n = (jnp.asarray(t) for t in _sample_bias_tables(w_buf, dec_seq))
    cache = pl.BlockSpec((None, None, D_A, w_buf), lambda b: (layer, b, 0, 0))
    new = pl.BlockSpec((None, NEW_PAD, D_A), lambda b: (b, 0, 0))
    o = pl.pallas_call(
        _attn_sample_kernel,
        grid=(batch,),
        in_specs=[pl.BlockSpec((None, rows, D_A), lambda b: (b, 0, 0)), new, new, cache, cache,
                  pl.BlockSpec(bias_w.shape, lambda b: (0, 0)), pl.BlockSpec(bias_n.shape, lambda b: (0, 0))],
        out_specs=pl.BlockSpec((None, dec_seq, D_A), lambda b: (b, 0, 0)),
        out_shape=jax.ShapeDtypeStruct((batch, dec_seq, D_A), F32),
        compiler_params=_cparams("parallel"),
        name="attn_sample",
    )(q_bd, pad_new(qkv_s[:, D_A:2 * D_A]), pad_new(qkv_s[:, 2 * D_A:]), cache_kt, cache_vt, bias_w, bias_n)
    return o.reshape(batch * dec_seq, D_A)


def _window_transposed(cache):
    depth, batch, w_buf = cache.shape[:3]
    return jnp.transpose(cache, (0, 1, 3, 4, 2)).reshape(depth, batch, D_A, w_buf)


def _sgu_sample_coef(ws, bias, dec_seq):
    assert 8 % dec_seq == 0
    t = np.arange(8) % dec_seq
    out = []
    for k in range(dec_seq):
        w = ws[:, :, t, np.maximum(t - k, 0)] * jnp.asarray(t >= k, F32)
        out.append(w)
    out.append(bias[:, :, t])
    coef = jnp.stack(out, axis=1)
    return jnp.repeat(jnp.swapaxes(coef, 2, 3), HEAD_DIM, axis=3)


def _sgu_sample_kernel(u_ref, v_ref, coef_ref, g_ref, b_ref, o_ref, vn_ref):
    n, width = u_ref.shape
    taps = coef_ref.shape[0] - 1
    u = _gelu(u_ref[...])
    vn = _layer_norm(_gelu(v_ref[...]), g_ref[...], b_ref[...])
    vn_ref[...] = vn
    z = jnp.broadcast_to(coef_ref[taps][None], (n // 8, 8, width))
    for k in range(taps):
        shifted = vn if k == 0 else pltpu.roll(vn, k, axis=0)
        z = z + coef_ref[k][None] * shifted.reshape(n // 8, 8, width)
    o_ref[...] = u * z.reshape(n, width)


def _sgu_sample(uv, first_row, n, coef, g, b, layer):
    blk = first_row // n
    vec = pl.BlockSpec((None, 1, D_B), lambda i: (layer, 0, 0))
    return pl.pallas_call(
        _sgu_sample_kernel,
        grid=(1,),
        in_specs=[pl.BlockSpec((n, D_B), lambda i: (blk, 0)), pl.BlockSpec((n, D_B), lambda i: (blk, 1)),
                  pl.BlockSpec((None,) + coef.shape[1:], lambda i: (layer, 0, 0, 0)), vec, vec],
        out_specs=[pl.BlockSpec((n, D_B), lambda i: (0, 0)), pl.BlockSpec((n, D_B), lambda i: (0, 0))],
        out_shape=[jax.ShapeDtypeStruct((n, D_B), F32), jax.ShapeDtypeStruct((n, D_B), F32)],
        compiler_params=_cparams("arbitrary"),
        name="sgu_sample",
    )(uv, uv, coef, g, b)


def _router_weights(r1, r1b, r2, r2b):
    depth = r1.shape[0]
    w = jnp.concatenate([r1, r2, jnp.zeros((depth, D_MODEL, LANES - N_GROUPS - N_EXPERTS), F32)], axis=-1)
    b = jnp.concatenate([r1b, r2b.reshape(depth, N_EXPERTS), jnp.zeros((depth, LANES - N_GROUPS - N_EXPERTS), F32)], axis=-1)
    return w, b[:, None, :]


def _route(x, wr, br):
    logits = _dot3(x, wr) + br
    lane_i = lax.broadcasted_iota(jnp.int32, logits.shape, 1)
    lane = lane_i.astype(F32)
    far = float(LANES)
    is_g = lane_i < N_GROUPS
    lg = jnp.where(is_g, logits, NEG)
    gmax = jnp.max(lg, axis=-1, keepdims=True)
    grp = jnp.min(jnp.where(lg == gmax, lane, far), axis=-1, keepdims=True)
    gate = 1.0 / jnp.sum(jnp.where(is_g, jnp.exp(lg - gmax), 0.0), axis=-1, keepdims=True)
    lo = N_GROUPS + EXPERTS_PER_GROUP * grp
    le = jnp.where((lane >= lo) & (lane < lo + EXPERTS_PER_GROUP), logits, NEG)
    t1 = jnp.max(le, axis=-1, keepdims=True)
    i1 = jnp.min(jnp.where(le == t1, lane, far), axis=-1, keepdims=True)
    le2 = jnp.where(lane == i1, NEG, le)
    t2 = jnp.max(le2, axis=-1, keepdims=True)
    i2 = jnp.min(jnp.where(le2 == t2, lane, far), axis=-1, keepdims=True)
    e = jnp.exp(t2 - t1)
    w1 = gate / (1.0 + e)
    w2 = gate * e / (1.0 + e)
    return jnp.where(lane_i == 0, i1 - N_GROUPS, jnp.where(lane_i == 1, i2 - N_GROUPS,
           jnp.where(lane_i == 2, w1, jnp.where(lane_i == 3, w2, 0.0))))


def _out_proj_kernel(oa_ref, ob_ref, oc_ref, x_ref, w_ref, g_ref, b_ref, wr_ref, br_ref, x1_ref, route_ref):
    mix = (jnp.dot(oa_ref[...].astype(BF16), w_ref[0:D_A, :], preferred_element_type=F32)
           + jnp.dot(ob_ref[...].astype(BF16), w_ref[D_A:D_A + D_B, :], preferred_element_type=F32)
           + jnp.dot(oc_ref[...].astype(BF16), w_ref[D_A + D_B:, :], preferred_element_type=F32))
    x1 = _layer_norm(DEEPNORM_ALPHA * x_ref[...] + mix, g_ref[...], b_ref[...])
    x1_ref[...] = x1
    route_ref[...] = _route(x1, wr_ref[...], br_ref[...])


def _out_proj(oa, ob, oc, x, w_out_bf16, g, b, wr, br, layer):
    n = x.shape[0]
    row = lambda w: pl.BlockSpec((ROW_TILE, w), lambda i: (i, 0))
    per_layer = lambda *shape: pl.BlockSpec((None,) + shape, lambda i: (layer,) + (0,) * len(shape))
    return pl.pallas_call(
        _out_proj_kernel,
        grid=(n // ROW_TILE,),
        in_specs=[row(D_A), row(D_B), row(D_C), row(D_MODEL), per_layer(D_MODEL, D_MODEL),
                  per_layer(1, D_MODEL), per_layer(1, D_MODEL), per_layer(D_MODEL, LANES), per_layer(1, LANES)],
        out_specs=[row(D_MODEL), row(LANES)],
        out_shape=[jax.ShapeDtypeStruct((n, D_MODEL), F32), jax.ShapeDtypeStruct((n, LANES), F32)],
        compiler_params=_cparams("parallel"),
        name="out_proj_ln_route",
    )(oa, ob, oc, x, w_out_bf16, g, b, wr, br)


def _moe_tiles(n_tokens):
    slots = 2 * n_tokens
    return -(-(slots + N_EXPERTS * (MOE_TILE - 1)) // MOE_TILE)


PLAN_BLOCK = 128


def _moe_plan(route, n_tokens):
    n_tiles = _moe_tiles(n_tokens)
    e = route[:, 0:2].astype(jnp.int32).reshape(-1, PLAN_BLOCK)
    onehot = (e[:, :, None] == jnp.arange(N_EXPERTS, dtype=jnp.int32)).astype(F32)
    tri = jnp.tril(jnp.ones((PLAN_BLOCK, PLAN_BLOCK), F32))
    within = jnp.einsum("ts,bse->bte", tri, onehot)
    block_total = within[:, -1, :]
    block_start = jnp.cumsum(block_total, axis=0) - block_total
    counts = (block_start[-1] + block_total[-1]).astype(jnp.int32)
    padded = (counts + MOE_TILE - 1) // MOE_TILE * MOE_TILE
    ends = jnp.cumsum(padded)
    starts = (ends - padded).astype(F32)
    pos = jnp.sum(onehot * (starts + block_start[:, None, :] + within - onehot), axis=-1)
    tile_start = jnp.arange(n_tiles, dtype=jnp.int32) * MOE_TILE
    tile_expert = jnp.sum((ends[None, :] <= tile_start[:, None]).astype(jnp.int32), axis=1)
    info = jnp.concatenate([ends, ends[-1:] // MOE_TILE]).astype(jnp.int32)
    return pos.astype(jnp.int32).reshape(-1), info, jnp.minimum(tile_expert, N_EXPERTS - 1)


def _dispatch_kernel(info_ref, pos_ref, x_ref, xs_hbm, zero_ref, sem, zero_sem, *, n_tiles):
    tokens = x_ref.shape[0]

    def zero_tile(row0):
        return pltpu.make_async_copy(zero_ref, xs_hbm.at[pl.ds(pl.multiple_of(row0, MOE_TILE), MOE_TILE), :], zero_sem)

    def for_each_pad_tile(act):
        for e in range(N_EXPERTS):
            start = info_ref[e - 1] if e else 0

            @pl.when(info_ref[e] > start)
            def _():
                act(zero_tile(info_ref[e] - MOE_TILE))

        def tail(t, _):
            act(zero_tile(t * MOE_TILE))
            return 0

        lax.fori_loop(info_ref[N_EXPERTS], n_tiles, tail, 0)

    @pl.when(pl.program_id(0) == 0)
    def _():
        zero_ref[...] = jnp.zeros_like(zero_ref)
        for_each_pad_tile(lambda cp: cp.start())
        for_each_pad_tile(lambda cp: cp.wait())

    def row_copy(t, dst_row):
        return pltpu.make_async_copy(x_ref.at[pl.ds(t, 1), :], xs_hbm.at[pl.ds(dst_row, 1), :], sem)

    def issue(t, _):
        row_copy(t, pos_ref[0, 0, 2 * t]).start()
        row_copy(t, pos_ref[0, 0, 2 * t + 1]).start()
        return 0

    def drain(t, _):
        row_copy(0, 0).wait()
        row_copy(0, 0).wait()
        return 0

    lax.fori_loop(0, tokens, issue, 0, unroll=4)
    lax.fori_loop(0, tokens, drain, 0, unroll=4)


def _dispatch(x1, pos, info):
    n = x1.shape[0]
    n_tiles = _moe_tiles(n)
    steps = n // ROW_TILE
    return pl.pallas_call(
        functools.partial(_dispatch_kernel, n_tiles=n_tiles),
        grid_spec=pltpu.PrefetchScalarGridSpec(
            num_scalar_prefetch=1,
            grid=(steps,),
            in_specs=[pl.BlockSpec((1, 1, 2 * ROW_TILE), lambda i, info: (i, 0, 0), memory_space=pltpu.SMEM),
                      pl.BlockSpec((ROW_TILE, D_MODEL), lambda i, info: (i, 0))],
            out_specs=pl.BlockSpec(memory_space=pl.ANY),
            scratch_shapes=[pltpu.VMEM((MOE_TILE, D_MODEL), F32), pltpu.SemaphoreType.DMA, pltpu.SemaphoreType.DMA]),
        out_shape=jax.ShapeDtypeStruct((n_tiles * MOE_TILE, D_MODEL), F32),
        compiler_params=_cparams("arbitrary"),
        name="moe_dispatch",
    )(info, pos.reshape(steps, 1, 2 * ROW_TILE), x1)


def _expert_kernel(te_ref, info_ref, x_ref, wg_ref, wu_ref, wd_ref, o_ref):
    used = pl.program_id(0) < info_ref[N_EXPERTS]

    @pl.when(used)
    def _():
        x = x_ref[...].astype(BF16)
        g = jnp.dot(x, wg_ref[...].astype(BF16), preferred_element_type=F32)
        u = jnp.dot(x, wu_ref[...].astype(BF16), preferred_element_type=F32)
        h = (g * _sigmoid(g)) * u
        o_ref[...] = jnp.dot(h.astype(BF16), wd_ref[...].astype(BF16), preferred_element_type=F32)

    @pl.when(jnp.logical_not(used))
    def _():
        o_ref[...] = jnp.zeros_like(o_ref)


def _experts(xs, tile_expert, info, w_gate, w_up, w_down, layer):
    rows = xs.shape[0]
    w_gate = w_gate.reshape(DEPTH * N_EXPERTS, D_MODEL, D_EXPERT)
    w_up = w_up.reshape(DEPTH * N_EXPERTS, D_MODEL, D_EXPERT)
    w_down = w_down.reshape(DEPTH * N_EXPERTS, D_EXPERT, D_MODEL)
    expert = lambda i, te, info: (layer * N_EXPERTS + te[i], 0, 0)
    return pl.pallas_call(
        _expert_kernel,
        grid_spec=pltpu.PrefetchScalarGridSpec(
            num_scalar_prefetch=2,
            grid=(rows // MOE_TILE,),
            in_specs=[pl.BlockSpec((MOE_TILE, D_MODEL), lambda i, te, info: (i, 0)),
                      pl.BlockSpec((None, D_MODEL, D_EXPERT), expert),
                      pl.BlockSpec((None, D_MODEL, D_EXPERT), expert),
                      pl.BlockSpec((None, D_EXPERT, D_MODEL), expert)],
            out_specs=pl.BlockSpec((MOE_TILE, D_MODEL), lambda i, te, info: (i, 0))),
        out_shape=jax.ShapeDtypeStruct((rows, D_MODEL), F32),
        compiler_params=_cparams("arbitrary"),
        name="experts",
    )(tile_expert, info, xs, w_gate, w_up, w_down)


COMBINE_TILE = 256


def _combine_ln_kernel(pos_ref, pos_next_ref, ys_hbm, x_ref, route_ref, g_ref, b_ref, o_ref, buf_ref, sem):
    tokens = x_ref.shape[0]
    i = pl.program_id(0)
    slot = i % 2

    def row_copy(src_row, dst_row, into):
        return pltpu.make_async_copy(ys_hbm.at[pl.ds(src_row, 1), :],
                                     buf_ref.at[into, pl.ds(dst_row, 1), :], sem.at[into])

    def issue(idx_ref, into):
        def body(t, _):
            row_copy(idx_ref[0, 0, 2 * t], t, into).start()
            row_copy(idx_ref[0, 0, 2 * t + 1], tokens + t, into).start()
            return 0
        lax.fori_loop(0, tokens, body, 0, unroll=4)

    @pl.when(i == 0)
    def _():
        issue(pos_ref, 0)

    @pl.when(i + 1 < pl.num_programs(0))
    def _():
        issue(pos_next_ref, 1 - slot)

    def drain(t, _):
        row_copy(0, 0, slot).wait()
        row_copy(0, 0, slot).wait()
        return 0

    lax.fori_loop(0, tokens, drain, 0, unroll=4)
    route = route_ref[...]
    y = route[:, 2:3] * buf_ref[slot, 0:tokens, :] + route[:, 3:4] * buf_ref[slot, tokens:, :]
    o_ref[...] = _layer_norm(DEEPNORM_ALPHA * x_ref[...] + y, g_ref[...], b_ref[...])


def _combine_ln(ys, pos, x1, route, g, b, layer):
    n = x1.shape[0]
    steps = n // COMBINE_TILE
    vec = pl.BlockSpec((None, 1, D_MODEL), lambda i: (layer, 0, 0))
    idx = lambda f: pl.BlockSpec((1, 1, 2 * COMBINE_TILE), lambda i: (f(i), 0, 0), memory_space=pltpu.SMEM)
    pos = pos.reshape(steps, 1, 2 * COMBINE_TILE)
    return pl.pallas_call(
        _combine_ln_kernel,
        grid=(steps,),
        in_specs=[idx(lambda i: i), idx(lambda i: jnp.minimum(i + 1, steps - 1)),
                  pl.BlockSpec(memory_space=pl.ANY),
                  pl.BlockSpec((COMBINE_TILE, D_MODEL), lambda i: (i, 0)),
                  pl.BlockSpec((COMBINE_TILE, LANES), lambda i: (i, 0)), vec, vec],
        out_specs=pl.BlockSpec((COMBINE_TILE, D_MODEL), lambda i: (i, 0)),
        out_shape=jax.ShapeDtypeStruct((n, D_MODEL), F32),
        scratch_shapes=[pltpu.VMEM((2, 2 * COMBINE_TILE, D_MODEL), F32), pltpu.SemaphoreType.DMA((2,))],
        compiler_params=_cparams("arbitrary"),
        name="moe_combine_ln",
    )(pos, pos, ys, x1, route, g, b)


def _moe(x1, route, p, layer):
    pos, info, tile_expert = _moe_plan(route, x1.shape[0])
    xs = _dispatch(x1, pos, info)
    ys = _experts(xs, tile_expert, info, p["moe_w_gate"], p["moe_w_up"], p["moe_w_down"], layer)
    return _combine_ln(ys, pos, x1, route, p["ln2_g3"], p["ln2_b3"], layer)


def kernel(x_prompt, x_sample, cache_win_k, cache_win_v, state_wkv, state_shift, w_in, w_out, sgu_ln_g, sgu_ln_b, sgu_ws, sgu_bias, rwkv_mu, rwkv_w0, rwkv_w2, rwkv_a0, rwkv_a2, rwkv_g2, rwkv_k_k, rwkv_k_a, rwkv_r_k, rwkv_lnx_g, rwkv_lnx_b, ln1_g, ln1_b, ln2_g, ln2_b, moe_router1, moe_router1_b, moe_router2, moe_router2_b, moe_w_gate, moe_w_up, moe_w_down):
    bp, tp, _ = x_prompt.shape
    bs, ts, _ = x_sample.shape
    n_p, n_s = bp * tp, bs * ts
    depth = w_in.shape[0]
    assert depth == DEPTH and tp <= DILATED_PATTERNS[-1][0] and tp % CHUNK == 0
    assert n_p % ROW_TILE == 0 and n_s % ROW_TILE == 0 and n_p % n_s == 0

    per_row = lambda a: a[:, None, :]
    p = dict(rwkv_mu=rwkv_mu, rwkv_w0=rwkv_w0, rwkv_a0=rwkv_a0, rwkv_k_k=rwkv_k_k, rwkv_k_a=rwkv_k_a,
             rwkv_r_k=rwkv_r_k, rwkv_lnx_g=rwkv_lnx_g, rwkv_lnx_b=rwkv_lnx_b,
             moe_w_gate=moe_w_gate, moe_w_up=moe_w_up, moe_w_down=moe_w_down,
             ln2_g3=per_row(ln2_g), ln2_b3=per_row(ln2_b))
    w_in_b = w_in.astype(BF16)
    w_kvt_b = jnp.swapaxes(w_in[:, :, D_A:3 * D_A], 1, 2).astype(BF16)
    w_out_b = w_out.astype(BF16)
    cache_kt, cache_vt = _window_transposed(cache_win_k), _window_transposed(cache_win_v)
    lora = _rwkv_lora_weights(rwkv_w2, rwkv_a2, rwkv_g2)
    wr, br = _router_weights(moe_router1, moe_router1_b, moe_router2, moe_router2_b)
    sgu_bias_full = _expand_sgu_bias(sgu_bias)
    sgu_coef = _sgu_sample_coef(sgu_ws, sgu_bias, ts)
    sgu_g, sgu_b = per_row(sgu_ln_g), per_row(sgu_ln_b)
    shift0 = state_shift[:, :, None, :]

    x = jnp.concatenate([x_prompt.reshape(n_p, D_MODEL), x_sample.reshape(n_s, D_MODEL)], axis=0)
    outs = [[] for _ in range(9)]
    for layer in range(depth):
        q_p, kt_p, vt_p, uv_p, pc_p = _in_proj_prompt(x, bp, tp, w_in_b, w_kvt_b, layer)
        qkv_s, uv_s, pc_s = _in_proj_sample(x, n_p, n_s, w_in_b, layer)
        vec = _rwkv_vec_table(p, layer)

        oa_p = _attn_prompt(q_p, kt_p, vt_p, bp, tp)
        ob_p = _sgu_prompt(uv_p, n_p, sgu_ws, sgu_bias_full, sgu_g, sgu_b, layer)
        oc_p, wkv_p = _rwkv_prompt(pc_p, bp, tp, lora[layer], vec)

        pc_s = pc_s.reshape(bs, ts, D_C_PROJ)
        oa_s = _attn_sample(qkv_s, cache_kt, cache_vt, layer, bs, ts)
        ob_s, vn_s = _sgu_sample(uv_s, 0, n_s, sgu_coef, sgu_g, sgu_b, layer)
        oc_s, wkv_s = _rwkv_sample(jnp.pad(pc_s, ((0, 0), (0, 8 - ts), (0, 0))), shift0, state_wkv,
                                   lora[layer], vec, layer, ts)
        oc_s = oc_s[:, :ts].reshape(n_s, D_C)

        x1, route = _out_proj(jnp.concatenate([oa_p, oa_s], axis=0), jnp.concatenate([ob_p, ob_s], axis=0),
                              jnp.concatenate([oc_p, oc_s], axis=0), x, w_out_b,
                              per_row(ln1_g), per_row(ln1_b), wr, br, layer)
        x = _moe(x1, route, p, layer)

        heads_p = lambda t: jnp.transpose(t.reshape(bp, N_HEADS_A, HEAD_DIM, tp), (0, 3, 1, 2))
        heads_s = lambda t: t.reshape(bs, ts, N_HEADS_A, HEAD_DIM)
        layer_outs = (heads_p(kt_p), heads_p(vt_p), wkv_p,
                      pc_p.reshape(bp, tp, D_C_PROJ)[:, -1],
                      heads_s(qkv_s[:, D_A:2 * D_A]), heads_s(qkv_s[:, 2 * D_A:]), wkv_s, pc_s[:, -1],
                      vn_s.reshape(bs, ts, D_B))
        for acc, o in zip(outs, layer_outs):
            acc.append(o)

    return (x[:n_p].reshape(bp, tp, D_MODEL), x[n_p:].reshape(bs, ts, D_MODEL)) + tuple(jnp.stack(o, 0) for o in outs)
```

```python
import functools
import math

import numpy as np
import jax
import jax.numpy as jnp
from jax import lax
from jax.experimental import pallas as pl
from jax.experimental.pallas import tpu as pltpu

F32 = jnp.float32
BF16 = jnp.bfloat16

D_MODEL = 1024
HEAD_DIM = 64
N_HEADS_A = 8
N_HEADS_B = 4
N_HEADS_C = 4
D_A = N_HEADS_A * HEAD_DIM
D_B = N_HEADS_B * HEAD_DIM
D_C = N_HEADS_C * HEAD_DIM
DILATED_PATTERNS = ((128, 1), (512, 4), (2048, 16))
CHUNK = 128
LORA_W, LORA_A, LORA_G = 32, 32, 64
D_LORA = LORA_W + LORA_A + LORA_G
D_C_PROJ = 3 * D_C + D_LORA
D_QKV = 3 * D_A
D_UV = 2 * D_B
D_IN = D_QKV + D_UV + D_C_PROJ
N_GROUPS = 4
EXPERTS_PER_GROUP = 8
N_EXPERTS = N_GROUPS * EXPERTS_PER_GROUP
D_EXPERT = 512
DEPTH = 2
DEEPNORM_ALPHA = (2 * DEPTH) ** 0.25
LN_EPS = 1e-5
GN_EPS = 64e-5
DECAY_SCALE = math.exp(-0.5)
L2_EPS = 1e-12
NEG = -1e30

LANES = 128
ROW_TILE = 512
Q_BLOCK = 128
RWKV_CHUNK = 64
MOE_TILE = 256
VMEM_LIMIT = 56 * 1024 * 1024


def _cparams(*sem):
    return pltpu.CompilerParams(dimension_semantics=sem, vmem_limit_bytes=VMEM_LIMIT)


def _dg(a, b, ca=1, cb=0):
    return lax.dot_general(a, b, (((ca,), (cb,)), ((), ())), preferred_element_type=F32)


def _split(x):
    hi = x.astype(BF16)
    lo = (x - hi.astype(F32)).astype(BF16)
    return hi, lo


def _dot3(a, b, ca=1, cb=0):
    ah, al = _split(a)
    bh, bl = _split(b)
    return _dg(ah, bh, ca, cb) + (_dg(ah, bl, ca, cb) + _dg(al, bh, ca, cb))


def _dot_exact_lhs(a_bf16, b, ca=1, cb=0):
    bh, bl = _split(b)
    return _dg(a_bf16, bh, ca, cb) + _dg(a_bf16, bl, ca, cb)


def _dot_exact_rhs(a, b_bf16, ca=1, cb=0):
    ah, al = _split(a)
    return _dg(ah, b_bf16, ca, cb) + _dg(al, b_bf16, ca, cb)


def _sigmoid(x):
    return 1.0 / (1.0 + jnp.exp(-x))


def _gelu(x):
    return 0.5 * x * (1.0 + lax.erf(x * (2.0 ** -0.5)))


def _layer_norm(z, g, b):
    mu = jnp.mean(z, axis=-1, keepdims=True)
    d = z - mu
    var = jnp.mean(d * d, axis=-1, keepdims=True)
    return d * lax.rsqrt(var + LN_EPS) * g + b


def _in_proj_sample_kernel(x_ref, w_ref, qkv_ref, uv_ref, pc_ref):
    x = x_ref[...].astype(BF16)
    qkv_ref[...] = jnp.dot(x, w_ref[:, 0:D_QKV], preferred_element_type=F32)
    uv_ref[...] = jnp.dot(x, w_ref[:, D_QKV:D_QKV + D_UV], preferred_element_type=F32)
    pc_ref[...] = jnp.dot(x, w_ref[:, D_QKV + D_UV:D_IN], preferred_element_type=F32)


def _in_proj_sample(x, first_row, n, w_in_bf16, layer):
    blk = first_row // n
    return pl.pallas_call(
        _in_proj_sample_kernel,
        grid=(1,),
        in_specs=[pl.BlockSpec((n, D_MODEL), lambda i: (blk, 0)),
                  pl.BlockSpec((None, D_MODEL, D_IN), lambda i: (layer, 0, 0))],
        out_specs=[pl.BlockSpec((n, w), lambda i: (0, 0)) for w in (D_QKV, D_UV, D_C_PROJ)],
        out_shape=[jax.ShapeDtypeStruct((n, w), F32) for w in (D_QKV, D_UV, D_C_PROJ)],
        compiler_params=_cparams("arbitrary"),
        name="in_proj_sample",
    )(x, w_in_bf16)


def _in_proj_prompt_kernel(x_ref, w_ref, wkvt_ref, q_ref, kt_ref, vt_ref, uv_ref, pc_ref):
    x = x_ref[...].astype(BF16)
    q_ref[...] = jnp.dot(x, w_ref[:, 0:D_A], preferred_element_type=F32)
    kt_ref[...] = _dg(wkvt_ref[0:D_A, :], x, 1, 1)
    vt_ref[...] = _dg(wkvt_ref[D_A:, :], x, 1, 1)
    uv_ref[...] = jnp.dot(x, w_ref[:, D_QKV:D_QKV + D_UV], preferred_element_type=F32)
    pc_ref[...] = jnp.dot(x, w_ref[:, D_QKV + D_UV:D_IN], preferred_element_type=F32)


def _in_proj_prompt(x, batch, seq, w_in_bf16, w_kvt_bf16, layer):
    n = batch * seq
    per_seq = seq // ROW_TILE
    row = lambda w: pl.BlockSpec((ROW_TILE, w), lambda i: (i, 0))
    tr = pl.BlockSpec((None, D_A, ROW_TILE), lambda i: (i // per_seq, 0, i % per_seq))
    return pl.pallas_call(
        _in_proj_prompt_kernel,
        grid=(n // ROW_TILE,),
        in_specs=[row(D_MODEL), pl.BlockSpec((None, D_MODEL, D_IN), lambda i: (layer, 0, 0)),
                  pl.BlockSpec((None, 2 * D_A, D_MODEL), lambda i: (layer, 0, 0))],
        out_specs=[row(D_A), tr, tr, row(D_UV), row(D_C_PROJ)],
        out_shape=[jax.ShapeDtypeStruct((n, D_A), F32), jax.ShapeDtypeStruct((batch, D_A, seq), F32),
                   jax.ShapeDtypeStruct((batch, D_A, seq), F32), jax.ShapeDtypeStruct((n, D_UV), F32),
                   jax.ShapeDtypeStruct((n, D_C_PROJ), F32)],
        compiler_params=_cparams("parallel"),
        name="in_proj_prompt",
    )(x, w_in_bf16, w_kvt_bf16)


def _pattern_count(delta):
    c = np.zeros(delta.shape, np.float64)
    for window, dilation in DILATED_PATTERNS:
        c += (delta >= 0) & (delta <= window) & (delta % dilation == 0)
    return c


def _log_count(delta):
    c = _pattern_count(delta)
    return np.where(c > 0, np.log(np.maximum(c, 1.0)), NEG).astype(np.float32)


def _prompt_bias_table(seq):
    i = np.arange(Q_BLOCK)[:, None]
    n = np.arange(seq)[None, :]
    tab = _log_count(seq - Q_BLOCK + i - n)
    return np.concatenate([tab, tab], axis=0)


def _attn_prompt_kernel(q_ref, kt_ref, vt_ref, bias_ref, o_ref, kb_ref, vb_ref, *, nb):
    seq = nb * Q_BLOCK
    kb_ref[...] = kt_ref[...].astype(BF16)
    vb_ref[...] = vt_ref[...].astype(BF16)
    first = lax.broadcasted_iota(jnp.int32, (Q_BLOCK, LANES), 1) < HEAD_DIM
    scale = HEAD_DIM ** -0.5
    for qb in range(nb):
        keys = (qb + 1) * Q_BLOCK
        q = q_ref[qb * Q_BLOCK:(qb + 1) * Q_BLOCK, :] * scale
        q2 = jnp.concatenate([jnp.where(first, q, 0.0), jnp.where(first, 0.0, q)], axis=0).astype(BF16)
        s = jnp.dot(q2, kb_ref[:, 0:keys], preferred_element_type=F32) + bias_ref[:, seq - keys:seq]
        p = jnp.exp(s - jnp.max(s, axis=-1, keepdims=True))
        denom = jnp.sum(p, axis=-1, keepdims=True)
        o = _dg(p.astype(BF16), vb_ref[:, 0:keys], 1, 1) / denom
        o_ref[qb * Q_BLOCK:(qb + 1) * Q_BLOCK, :] = jnp.where(first, o[:Q_BLOCK], o[Q_BLOCK:])


def _attn_prompt(q, kt, vt, batch, seq):
    nb = seq // Q_BLOCK
    pairs = D_A // LANES
    bias = jnp.asarray(_prompt_bias_table(seq))
    tr = pl.BlockSpec((None, LANES, seq), lambda b, p: (b, p, 0))
    return pl.pallas_call(
        functools.partial(_attn_prompt_kernel, nb=nb),
        grid=(batch, pairs),
        in_specs=[pl.BlockSpec((seq, LANES), lambda b, p: (b, p)), tr, tr,
                  pl.BlockSpec((2 * Q_BLOCK, seq), lambda b, p: (0, 0))],
        out_specs=pl.BlockSpec((seq, LANES), lambda b, p: (b, p)),
        out_shape=jax.ShapeDtypeStruct((batch * seq, D_A), F32),
        scratch_shapes=[pltpu.VMEM((LANES, seq), BF16), pltpu.VMEM((LANES, seq), BF16)],
        compiler_params=_cparams("parallel", "parallel"),
        name="attn_prompt",
    )(q, kt, vt, bias)


def _sgu_prompt_kernel(u_ref, v_ref, ws_ref, bias_ref, g_ref, b_ref, o_ref):
    u = _gelu(u_ref[...])
    vn = _layer_norm(_gelu(v_ref[...]), g_ref[...], b_ref[...]).astype(BF16)
    r = lax.broadcasted_iota(jnp.int32, (CHUNK, CHUNK), 0)
    c = lax.broadcasted_iota(jnp.int32, (CHUNK, CHUNK), 1)
    head = lax.broadcasted_iota(jnp.int32, (CHUNK, D_B), 1) // HEAD_DIM
    wm = [jnp.where(r >= c, ws_ref[h], 0.0).astype(BF16) for h in range(N_HEADS_B)]
    for j in range(u_ref.shape[0] // CHUNK):
        rows = slice(j * CHUNK, (j + 1) * CHUNK)
        z = bias_ref[...]
        for h in range(N_HEADS_B):
            z = z + jnp.where(head == h, jnp.dot(wm[h], vn[rows], preferred_element_type=F32), 0.0)
        o_ref[rows, :] = u[rows] * z


def _expand_sgu_bias(bias):
    return jnp.repeat(jnp.swapaxes(bias, 1, 2), HEAD_DIM, axis=2)


def _sgu_prompt(uv, n, ws, bias_full, g, b, layer):
    vec = pl.BlockSpec((None, 1, D_B), lambda i: (layer, 0, 0))
    rows = math.gcd(n, ROW_TILE)
    return pl.pallas_call(
        _sgu_prompt_kernel,
        grid=(n // rows,),
        in_specs=[pl.BlockSpec((rows, D_B), lambda i: (i, 0)), pl.BlockSpec((rows, D_B), lambda i: (i, 1)),
                  pl.BlockSpec((None, N_HEADS_B, CHUNK, CHUNK), lambda i: (layer, 0, 0, 0)),
                  pl.BlockSpec((None, CHUNK, D_B), lambda i: (layer, 0, 0)), vec, vec],
        out_specs=pl.BlockSpec((rows, D_B), lambda i: (i, 0)),
        out_shape=jax.ShapeDtypeStruct((n, D_B), F32),
        compiler_params=_cparams("parallel"),
        name="sgu_prompt",
    )(uv, uv, ws, bias_full, g, b)


def _rwkv_lora_weights(w2, a2, g2):
    depth = w2.shape[0]
    w = jnp.zeros((depth, D_LORA, 3 * D_C), F32)
    w = w.at[:, 0:LORA_W, 0:D_C].set(w2)
    w = w.at[:, LORA_W:LORA_W + LORA_A, D_C:2 * D_C].set(a2)
    return w.at[:, LORA_W + LORA_A:, 2 * D_C:].set(g2)


def _rwkv_features(xs, wl, vec):
    r = xs[:, 0:D_C]
    k = xs[:, D_C:2 * D_C]
    v = xs[:, 2 * D_C:3 * D_C]
    tail = xs[:, 3 * D_C:]
    lane = lax.broadcasted_iota(jnp.int32, tail.shape, 1)
    feats = jnp.where(lane < LORA_W, jnp.tanh(tail), jnp.where(lane < LORA_W + LORA_A, tail, _sigmoid(tail)))
    lora = _dot3(feats, wl)
    logw = -DECAY_SCALE * _sigmoid(vec["w0"] + lora[:, 0:D_C])
    a = _sigmoid(vec["a0"] + lora[:, D_C:2 * D_C])
    gate = lora[:, 2 * D_C:]
    kk = k * vec["k_k"]
    ri = lax.broadcasted_iota(jnp.int32, (D_C, D_C), 0) // HEAD_DIM
    ci = lax.broadcasted_iota(jnp.int32, (D_C, D_C), 1) // HEAD_DIM
    head_ones = jnp.where(ri == ci, 1.0, 0.0).astype(BF16)
    kk = kk / jnp.maximum(jnp.sqrt(_dot_exact_rhs(kk * kk, head_ones)), L2_EPS)
    k = k * (1.0 + (a - 1.0) * vec["k_a"])
    bonus = _dot_exact_rhs(r * k * vec["r_k"], head_ones) * v
    return r, k, v, logw, kk, kk * a, gate, bonus


def _rwkv_chunk_fn(E, C, n_valid, wl_ref, vec_ref):
    H = N_HEADS_C
    HC = H * C
    n = E * HC
    names = ("mu_r", "mu_k", "mu_v", "w0", "a0", "k_k", "k_a", "r_k", "lnx_g", "lnx_b")
    vec = {nm: vec_ref[i:i + 1, :] for i, nm in enumerate(names)}
    mu = jnp.concatenate([vec["mu_r"], vec["mu_k"], vec["mu_v"], vec_ref[len(names):len(names) + 1, 0:D_LORA]], axis=1)

    first_row = lax.broadcasted_iota(jnp.int32, (E * C, D_C_PROJ), 0) % C == 0
    is_token = lax.broadcasted_iota(jnp.int32, (E * C, D_C), 0) % C < n_valid
    srow = lax.broadcasted_iota(jnp.int32, (n, n), 0)
    scol = lax.broadcasted_iota(jnp.int32, (n, n), 1)
    same_block = (srow // C) == (scol // C)
    strict = same_block & ((srow % C) > (scol % C))
    incl = same_block & ((srow % C) >= (scol % C))
    eye = jnp.where(srow == scol, 1.0, 0.0)
    own = (lax.broadcasted_iota(jnp.int32, (1, H, 1, D_C), 1)
           == lax.broadcasted_iota(jnp.int32, (1, H, 1, D_C), 3) // HEAD_DIM)
    own_rows = jnp.broadcast_to(own, (E, H, C, D_C)).reshape(n, D_C)
    tr = lax.broadcasted_iota(jnp.int32, (E * C, E * C), 0)
    tc = lax.broadcasted_iota(jnp.int32, (E * C, E * C), 1)
    cum = jnp.where(((tr // C) == (tc // C)) & (tr >= tc), 1.0, 0.0).astype(BF16)

    def stack(x):
        x4 = jnp.broadcast_to(x.reshape(E, 1, C, D_C), (E, H, C, D_C))
        return jnp.where(own, x4, 0.0).reshape(n, D_C)

    def unstack(xs):
        x4 = xs.reshape(E, H, C, D_C)
        return ((x4[:, 0] + x4[:, 1]) + (x4[:, 2] + x4[:, 3])).reshape(E * C, D_C)

    def per_sequence(x, f):
        return jnp.broadcast_to(f(x.reshape(E, C, D_C)), (E, C, D_C)).reshape(E * C, D_C)

    def chunk(pc, prev_rows, s0):
        prev0 = jnp.broadcast_to(prev_rows, (E, C, D_C_PROJ)).reshape(E * C, D_C_PROJ)
        prev = jnp.where(first_row, prev0, pltpu.roll(pc, 1, axis=0))
        xs = pc + (prev - pc) * mu
        r, k, v, logw, kk, b, gate, bonus = _rwkv_features(xs, wl_ref[...], vec)
        if n_valid < C:
            logw = jnp.where(is_token, logw, 0.0)
            kk = jnp.where(is_token, kk, 0.0)
            b = jnp.where(is_token, b, 0.0)
            k = jnp.where(is_token, k, 0.0)

        lc = _dot_exact_lhs(cum, logw)
        lc_end = per_sequence(lc, lambda t: t[:, C - 1:C, :])
        w_inv = jnp.exp(-lc)
        a_t = stack(-kk * jnp.exp(lc - logw))
        r_t = stack(r * jnp.exp(lc))
        b_t = stack(b * w_inv)
        k_t = stack(k * w_inv)
        to_end = jnp.exp(lc_end - lc)
        b_e = stack(b * to_end)
        k_e = stack(k * to_end)
        v_s = stack(v)

        ar = jnp.concatenate([a_t, r_t], axis=0).astype(BF16)
        gram = _dg(ar, jnp.concatenate([b_t, k_t], axis=0).astype(BF16), 1, 1)
        ab = jnp.where(strict, gram[:n, :n], 0.0)
        ak = jnp.where(strict, gram[:n, n:], 0.0).astype(BF16)
        rb = jnp.where(incl, gram[n:, :n], 0.0).astype(BF16)
        rk = jnp.where(incl, gram[n:, n:], 0.0).astype(BF16)

        inv = eye + ab
        power = ab.astype(BF16)
        for step in range(int(math.log2(C)) - 1):
            power = _dg(power, power)
            inv = inv + _dg(power.astype(BF16), inv.astype(BF16))
            power = power.astype(BF16)

        seq_rows = lambda x, e, off=0: x[off + e * HC:off + (e + 1) * HC]
        from_s0 = [_dg(jnp.concatenate([seq_rows(ar, e), seq_rows(ar, e, n)], axis=0), s0[e].astype(BF16), 1, 1)
                   for e in range(E)]
        a_s0 = jnp.concatenate([f[:HC] for f in from_s0], axis=0)
        r_s0 = jnp.concatenate([f[HC:] for f in from_s0], axis=0)
        v_b = v_s.astype(BF16)
        u_b = _dg(inv.astype(BF16), (a_s0 + _dg(ak, v_b)).astype(BF16)).astype(BF16)
        y = unstack(r_s0 + _dg(rb, u_b) + _dg(rk, v_b))
        b_e = b_e.astype(BF16)
        k_e = k_e.astype(BF16)
        s_end = [s0[e] * jnp.exp(lc_end[e * C:e * C + 1]) + _dg(seq_rows(u_b, e), seq_rows(b_e, e), 0, 0)
                 + _dg(seq_rows(v_b, e), seq_rows(k_e, e), 0, 0) for e in range(E)]

        ys = stack(y)
        mean = jnp.sum(ys, axis=-1, keepdims=True) * (1.0 / HEAD_DIM)
        d = jnp.where(own_rows, ys - mean, 0.0)
        var = jnp.sum(d * d, axis=-1, keepdims=True) * (1.0 / HEAD_DIM)
        yn = unstack(d * lax.rsqrt(var + GN_EPS)) * vec["lnx_g"] + vec["lnx_b"]
        return (yn + bonus) * gate, s_end

    return chunk


def _store_head_states(s_out_ref, s):
    for h in range(N_HEADS_C):
        s_out_ref[h] = s[h * HEAD_DIM:(h + 1) * HEAD_DIM, h * HEAD_DIM:(h + 1) * HEAD_DIM]


def _rwkv_prompt_kernel(pc_ref, wl_ref, vec_ref, o_ref, s_out_ref, s_ref, prev_ref):
    C = RWKV_CHUNK
    group, span, _ = pc_ref.shape
    chunk = _rwkv_chunk_fn(1, C, C, wl_ref, vec_ref)

    @pl.when(pl.program_id(1) == 0)
    def _():
        s_ref[...] = jnp.zeros_like(s_ref)
        prev_ref[...] = jnp.zeros_like(prev_ref)

    def step(c, _):
        t0 = pl.multiple_of(c * C, C)
        for g in range(group):
            pc = pc_ref[g, pl.ds(t0, C), :]
            out, (s_end,) = chunk(pc, prev_ref[g][None], [s_ref[g]])
            prev_ref[g] = pc[C - 1:C, :]
            s_ref[g] = s_end
            o_ref[g, pl.ds(t0, C), :] = out
        return 0

    lax.fori_loop(0, span // C, step, 0)

    @pl.when(pl.program_id(1) == pl.num_programs(1) - 1)
    def _():
        for g in range(group):
            _store_head_states(s_out_ref.at[g], s_ref[g])


def _rwkv_sample_kernel(pc_ref, shift_ref, s_in_ref, wl_ref, vec_ref, o_ref, s_out_ref, *, n_valid):
    group, C, _ = pc_ref.shape
    zero = jnp.zeros((HEAD_DIM, HEAD_DIM), F32)
    s0 = [jnp.concatenate(
        [jnp.concatenate([s_in_ref[e, h] if g == h else zero for g in range(N_HEADS_C)], axis=1)
         for h in range(N_HEADS_C)], axis=0) for e in range(group)]
    chunk = _rwkv_chunk_fn(group, C, n_valid, wl_ref, vec_ref)
    out, s_end = chunk(pc_ref[...].reshape(group * C, D_C_PROJ), shift_ref[...], s0)
    o_ref[...] = out.reshape(group, C, D_C)
    for e in range(group):
        _store_head_states(s_out_ref.at[e], s_end[e])


RWKV_SAMPLE_GROUP = 8


def _rwkv_sample(pc_pad, shift0, wkv0, wl, vec, layer, n_valid):
    batch, rows, _ = pc_pad.shape
    group = math.gcd(batch, RWKV_SAMPLE_GROUP)
    state = pl.BlockSpec((None, group, N_HEADS_C, HEAD_DIM, HEAD_DIM), lambda b: (layer, b, 0, 0, 0))
    return pl.pallas_call(
        functools.partial(_rwkv_sample_kernel, n_valid=n_valid),
        grid=(batch // group,),
        in_specs=[pl.BlockSpec((group, rows, D_C_PROJ), lambda b: (b, 0, 0)),
                  pl.BlockSpec((None, group, 1, D_C_PROJ), lambda b: (layer, b, 0, 0)),
                  state,
                  pl.BlockSpec((D_LORA, 3 * D_C), lambda b: (0, 0)),
                  pl.BlockSpec((16, D_C), lambda b: (0, 0))],
        out_specs=[pl.BlockSpec((group, rows, D_C), lambda b: (b, 0, 0)),
                   pl.BlockSpec((group, N_HEADS_C, HEAD_DIM, HEAD_DIM), lambda b: (b, 0, 0, 0))],
        out_shape=[jax.ShapeDtypeStruct((batch, rows, D_C), F32),
                   jax.ShapeDtypeStruct((batch, N_HEADS_C, HEAD_DIM, HEAD_DIM), F32)],
        compiler_params=_cparams("parallel"),
        name="rwkv_sample",
    )(pc_pad, shift0, wkv0, wl, vec)


def _rwkv_vec_table(p, layer):
    mu = p["rwkv_mu"][layer]
    rows = [mu[0:D_C], mu[D_C:2 * D_C], mu[2 * D_C:3 * D_C], p["rwkv_w0"][layer], p["rwkv_a0"][layer],
            p["rwkv_k_k"][layer], p["rwkv_k_a"][layer], p["rwkv_r_k"][layer].reshape(D_C),
            p["rwkv_lnx_g"][layer], p["rwkv_lnx_b"][layer], jnp.pad(mu[3 * D_C:], (0, D_C - D_LORA))]
    rows += [jnp.zeros((D_C,), F32)] * (16 - len(rows))
    return jnp.stack(rows, 0)


RWKV_GROUP = 8
RWKV_SPAN = 256


def _rwkv_prompt(pc, batch, seq, wl, vec):
    group = math.gcd(batch, RWKV_GROUP)
    span = math.gcd(seq, RWKV_SPAN)
    o, s = pl.pallas_call(
        _rwkv_prompt_kernel,
        grid=(batch // group, seq // span),
        in_specs=[pl.BlockSpec((group, span, D_C_PROJ), lambda b, t: (b, t, 0)),
                  pl.BlockSpec((D_LORA, 3 * D_C), lambda b, t: (0, 0)),
                  pl.BlockSpec((16, D_C), lambda b, t: (0, 0))],
        out_specs=[pl.BlockSpec((group, span, D_C), lambda b, t: (b, t, 0)),
                   pl.BlockSpec((group, N_HEADS_C, HEAD_DIM, HEAD_DIM), lambda b, t: (b, 0, 0, 0))],
        out_shape=[jax.ShapeDtypeStruct((batch, seq, D_C), F32),
                   jax.ShapeDtypeStruct((batch, N_HEADS_C, HEAD_DIM, HEAD_DIM), F32)],
        scratch_shapes=[pltpu.VMEM((group, D_C, D_C), F32), pltpu.VMEM((group, 1, D_C_PROJ), F32)],
        compiler_params=_cparams("parallel", "arbitrary"),
        name="rwkv_prompt",
    )(pc[:batch * seq].reshape(batch, seq, D_C_PROJ), wl, vec)
    return o.reshape(batch * seq, D_C), s


NEW_PAD = 8


def _sample_bias_tables(w_buf, dec_seq):
    q_pos = w_buf + np.repeat(np.arange(dec_seq), N_HEADS_A)[:, None]
    window = _log_count(q_pos - np.arange(w_buf)[None, :])
    new = _log_count(q_pos - (w_buf + np.arange(NEW_PAD))[None, :])
    new[:, dec_seq:] = NEG
    return window, new


def _attn_sample_kernel(q_ref, knew_ref, vnew_ref, kt_ref, vt_ref, bias_w_ref, bias_n_ref, o_ref):
    rows = q_ref.shape[0]
    dec_seq = rows // N_HEADS_A
    q = q_ref[...].astype(BF16)
    s_w = jnp.dot(q, kt_ref[...].astype(BF16), preferred_element_type=F32) + bias_w_ref[...]
    s_n = _dg(q, knew_ref[...].astype(BF16), 1, 1) + bias_n_ref[...]
    m = jnp.maximum(jnp.max(s_w, axis=-1, keepdims=True), jnp.max(s_n, axis=-1, keepdims=True))
    p_w = jnp.exp(s_w - m)
    p_n = jnp.exp(s_n - m)
    denom = jnp.sum(p_w, axis=-1, keepdims=True) + jnp.sum(p_n, axis=-1, keepdims=True)
    o = (_dg(p_w.astype(BF16), vt_ref[...].astype(BF16), 1, 1)
         + jnp.dot(p_n.astype(BF16), vnew_ref[...].astype(BF16), preferred_element_type=F32)) / denom
    head = lax.broadcasted_iota(jnp.int32, (N_HEADS_A, D_A), 0)
    col_head = lax.broadcasted_iota(jnp.int32, (N_HEADS_A, D_A), 1) // HEAD_DIM
    for i in range(dec_seq):
        tile = o[i * N_HEADS_A:(i + 1) * N_HEADS_A, :]
        o_ref[i:i + 1, :] = jnp.sum(jnp.where(head == col_head, tile, 0.0), axis=0, keepdims=True)


def _attn_sample(qkv_s, cache_kt, cache_vt, layer, batch, dec_seq):
    w_buf = cache_kt.shape[3]
    assert dec_seq <= NEW_PAD
    rows = dec_seq * N_HEADS_A
    q = qkv_s[:, 0:D_A].reshape(batch, dec_seq, N_HEADS_A, 1, HEAD_DIM) * HEAD_DIM ** -0.5
    q_bd = (q * jnp.eye(N_HEADS_A, dtype=F32)[None, None, :, :, None]).reshape(batch, rows, D_A)
    pad_new = lambda t: jnp.pad(t.reshape(batch, dec_seq, D_A), ((0, 0), (0, NEW_PAD - dec_seq), (0, 0)))
    bias_w, bias_n = (jnp.asarray(t) for t in _sample_bias_tables(w_buf, dec_seq))
    cache = pl.BlockSpec((None, None, D_A, w_buf), lambda b: (layer, b, 0, 0))
    new = pl.BlockSpec((None, NEW_PAD, D_A), lambda b: (b, 0, 0))
    o = pl.pallas_call(
        _attn_sample_kernel,
        grid=(batch,),
        in_specs=[pl.BlockSpec((None, rows, D_A), lambda b: (b, 0, 0)), new, new, cache, cache,
                  pl.BlockSpec(bias_w.shape, lambda b: (0, 0)), pl.BlockSpec(bias_n.shape, lambda b: (0, 0))],
        out_specs=pl.BlockSpec((None, dec_seq, D_A), lambda b: (b, 0, 0)),
        out_shape=jax.ShapeDtypeStruct((batch, dec_seq, D_A), F32),
        compiler_params=_cparams("parallel"),
        name="attn_sample",
    )(q_bd, pad_new(qkv_s[:, D_A:2 * D_A]), pad_new(qkv_s[:, 2 * D_A:]), cache_kt, cache_vt, bias_w, bias_n)
    return o.reshape(batch * dec_seq, D_A)


def _window_transposed(cache):
    depth, batch, w_buf = cache.shape[:3]
    return jnp.transpose(cache, (0, 1, 3, 4, 2)).reshape(depth, batch, D_A, w_buf)


def _sgu_sample_coef(ws, bias, dec_seq):
    assert 8 % dec_seq == 0
    t = np.arange(8) % dec_seq
    out = []
    for k in range(dec_seq):
        w = ws[:, :, t, np.maximum(t - k, 0)] * jnp.asarray(t >= k, F32)
        out.append(w)
    out.append(bias[:, :, t])
    coef = jnp.stack(out, axis=1)
    return jnp.repeat(jnp.swapaxes(coef, 2, 3), HEAD_DIM, axis=3)


def _sgu_sample_kernel(u_ref, v_ref, coef_ref, g_ref, b_ref, o_ref, vn_ref):
    n, width = u_ref.shape
    taps = coef_ref.shape[0] - 1
    u = _gelu(u_ref[...])
    vn = _layer_norm(_gelu(v_ref[...]), g_ref[...], b_ref[...])
    vn_ref[...] = vn
    z = jnp.broadcast_to(coef_ref[taps][None], (n // 8, 8, width))
    for k in range(taps):
        shifted = vn if k == 0 else pltpu.roll(vn, k, axis=0)
        z = z + coef_ref[k][None] * shifted.reshape(n // 8, 8, width)
    o_ref[...] = u * z.reshape(n, width)


def _sgu_sample(uv, first_row, n, coef, g, b, layer):
    blk = first_row // n
    vec = pl.BlockSpec((None, 1, D_B), lambda i: (layer, 0, 0))
    return pl.pallas_call(
        _sgu_sample_kernel,
        grid=(1,),
        in_specs=[pl.BlockSpec((n, D_B), lambda i: (blk, 0)), pl.BlockSpec((n, D_B), lambda i: (blk, 1)),
                  pl.BlockSpec((None,) + coef.shape[1:], lambda i: (layer, 0, 0, 0)), vec, vec],
        out_specs=[pl.BlockSpec((n, D_B), lambda i: (0, 0)), pl.BlockSpec((n, D_B), lambda i: (0, 0))],
        out_shape=[jax.ShapeDtypeStruct((n, D_B), F32), jax.ShapeDtypeStruct((n, D_B), F32)],
        compiler_params=_cparams("arbitrary"),
        name="sgu_sample",
    )(uv, uv, coef, g, b)


def _router_weights(r1, r1b, r2, r2b):
    depth = r1.shape[0]
    w = jnp.concatenate([r1, r2, jnp.zeros((depth, D_MODEL, LANES - N_GROUPS - N_EXPERTS), F32)], axis=-1)
    b = jnp.concatenate([r1b, r2b.reshape(depth, N_EXPERTS), jnp.zeros((depth, LANES - N_GROUPS - N_EXPERTS), F32)], axis=-1)
    return w, b[:, None, :]


def _route(x, wr, br):
    logits = _dot3(x, wr) + br
    lane_i = lax.broadcasted_iota(jnp.int32, logits.shape, 1)
    lane = lane_i.astype(F32)
    far = float(LANES)
    is_g = lane_i < N_GROUPS
    lg = jnp.where(is_g, logits, NEG)
    gmax = jnp.max(lg, axis=-1, keepdims=True)
    grp = jnp.min(jnp.where(lg == gmax, lane, far), axis=-1, keepdims=True)
    gate = 1.0 / jnp.sum(jnp.where(is_g, jnp.exp(lg - gmax), 0.0), axis=-1, keepdims=True)
    lo = N_GROUPS + EXPERTS_PER_GROUP * grp
    le = jnp.where((lane >= lo) & (lane < lo + EXPERTS_PER_GROUP), logits, NEG)
    t1 = jnp.max(le, axis=-1, keepdims=True)
    i1 = jnp.min(jnp.where(le == t1, lane, far), axis=-1, keepdims=True)
    le2 = jnp.where(lane == i1, NEG, le)
    t2 = jnp.max(le2, axis=-1, keepdims=True)
    i2 = jnp.min(jnp.where(le2 == t2, lane, far), axis=-1, keepdims=True)
    e = jnp.exp(t2 - t1)
    w1 = gate / (1.0 + e)
    w2 = gate * e / (1.0 + e)
    return jnp.where(lane_i == 0, i1 - N_GROUPS, jnp.where(lane_i == 1, i2 - N_GROUPS,
           jnp.where(lane_i == 2, w1, jnp.where(lane_i == 3, w2, 0.0))))


def _out_proj_kernel(oa_p_ref, ob_p_ref, oc_p_ref, oa_s_ref, ob_s_ref, oc_s_ref, x_ref, w_ref, g_ref, b_ref,
                     wr_ref, br_ref, x1_ref, route_ref, *, prompt_tiles):
    prompt = pl.program_id(0) < prompt_tiles
    pick = lambda p_ref, s_ref: jnp.where(prompt, p_ref[...], s_ref[...]).astype(BF16)
    mix = (jnp.dot(pick(oa_p_ref, oa_s_ref), w_ref[0:D_A, :], preferred_element_type=F32)
           + jnp.dot(pick(ob_p_ref, ob_s_ref), w_ref[D_A:D_A + D_B, :], preferred_element_type=F32)
           + jnp.dot(pick(oc_p_ref, oc_s_ref), w_ref[D_A + D_B:, :], preferred_element_type=F32))
    x1 = _layer_norm(DEEPNORM_ALPHA * x_ref[...] + mix, g_ref[...], b_ref[...])
    x1_ref[...] = x1
    route_ref[...] = _route(x1, wr_ref[...], br_ref[...])


def _out_proj(mix_p, mix_s, x, w_out_bf16, g, b, wr, br, layer):
    n = x.shape[0]
    prompt_tiles = mix_p[0].shape[0] // ROW_TILE
    row = lambda w: pl.BlockSpec((ROW_TILE, w), lambda i: (i, 0))
    prompt_row = lambda w: pl.BlockSpec((ROW_TILE, w), lambda i: (jnp.minimum(i, prompt_tiles - 1), 0))
    sample_row = lambda w: pl.BlockSpec((ROW_TILE, w), lambda i: (jnp.maximum(i - prompt_tiles, 0), 0))
    per_layer = lambda *shape: pl.BlockSpec((None,) + shape, lambda i: (layer,) + (0,) * len(shape))
    widths = (D_A, D_B, D_C)
    return pl.pallas_call(
        functools.partial(_out_proj_kernel, prompt_tiles=prompt_tiles),
        grid=(n // ROW_TILE,),
        in_specs=[prompt_row(w) for w in widths] + [sample_row(w) for w in widths]
                 + [row(D_MODEL), per_layer(D_MODEL, D_MODEL), per_layer(1, D_MODEL), per_layer(1, D_MODEL),
                    per_layer(D_MODEL, LANES), per_layer(1, LANES)],
        out_specs=[row(D_MODEL), row(LANES)],
        out_shape=[jax.ShapeDtypeStruct((n, D_MODEL), F32), jax.ShapeDtypeStruct((n, LANES), F32)],
        compiler_params=_cparams("parallel"),
        name="out_proj_ln_route",
    )(*mix_p, *mix_s, x, w_out_bf16, g, b, wr, br)


def _moe_tiles(n_tokens):
    slots = 2 * n_tokens
    return -(-(slots + N_EXPERTS * (MOE_TILE - 1)) // MOE_TILE)


PLAN_BLOCK = 128


def _moe_plan(route, n_tokens):
    n_tiles = _moe_tiles(n_tokens)
    e = route[:, 0:2].astype(jnp.int32).reshape(-1, PLAN_BLOCK)
    onehot = (e[:, :, None] == jnp.arange(N_EXPERTS, dtype=jnp.int32)).astype(F32)
    tri = jnp.tril(jnp.ones((PLAN_BLOCK, PLAN_BLOCK), F32))
    within = jnp.einsum("ts,bse->bte", tri, onehot)
    block_total = within[:, -1, :]
    block_start = jnp.cumsum(block_total, axis=0) - block_total
    counts = (block_start[-1] + block_total[-1]).astype(jnp.int32)
    padded = (counts + MOE_TILE - 1) // MOE_TILE * MOE_TILE
    ends = jnp.cumsum(padded)
    starts = (ends - padded).astype(F32)
    pos = jnp.sum(onehot * (starts + block_start[:, None, :] + within - onehot), axis=-1)
    tile_start = jnp.arange(n_tiles, dtype=jnp.int32) * MOE_TILE
    tile_expert = jnp.sum((ends[None, :] <= tile_start[:, None]).astype(jnp.int32), axis=1)
    info = jnp.concatenate([ends, ends[-1:] // MOE_TILE]).astype(jnp.int32)
    return pos.astype(jnp.int32).reshape(-1), info, jnp.minimum(tile_expert, N_EXPERTS - 1)


def _dispatch_kernel(info_ref, pos_ref, x_ref, xs_hbm, zero_ref, sem, zero_sem, *, n_tiles):
    tokens = x_ref.shape[0]

    def zero_tile(row0):
        return pltpu.make_async_copy(zero_ref, xs_hbm.at[pl.ds(pl.multiple_of(row0, MOE_TILE), MOE_TILE), :], zero_sem)

    def for_each_pad_tile(act):
        for e in range(N_EXPERTS):
            start = info_ref[e - 1] if e else 0

            @pl.when(info_ref[e] > start)
            def _():
                act(zero_tile(info_ref[e] - MOE_TILE))

        def tail(t, _):
            act(zero_tile(t * MOE_TILE))
            return 0

        lax.fori_loop(info_ref[N_EXPERTS], n_tiles, tail, 0)

    @pl.when(pl.program_id(0) == 0)
    def _():
        zero_ref[...] = jnp.zeros_like(zero_ref)
        for_each_pad_tile(lambda cp: cp.start())
        for_each_pad_tile(lambda cp: cp.wait())

    def row_copy(t, dst_row):
        return pltpu.make_async_copy(x_ref.at[pl.ds(t, 1), :], xs_hbm.at[pl.ds(dst_row, 1), :], sem)

    def issue(t, _):
        row_copy(t, pos_ref[0, 0, 2 * t]).start()
        row_copy(t, pos_ref[0, 0, 2 * t + 1]).start()
        return 0

    def drain(t, _):
        row_copy(0, 0).wait()
        row_copy(0, 0).wait()
        return 0

    lax.fori_loop(0, tokens, issue, 0, unroll=4)
    lax.fori_loop(0, tokens, drain, 0, unroll=4)


def _dispatch(x1, pos, info):
    n = x1.shape[0]
    n_tiles = _moe_tiles(n)
    steps = n // ROW_TILE
    return pl.pallas_call(
        functools.partial(_dispatch_kernel, n_tiles=n_tiles),
        grid_spec=pltpu.PrefetchScalarGridSpec(
            num_scalar_prefetch=1,
            grid=(steps,),
            in_specs=[pl.BlockSpec((1, 1, 2 * ROW_TILE), lambda i, info: (i, 0, 0), memory_space=pltpu.SMEM),
                      pl.BlockSpec((ROW_TILE, D_MODEL), lambda i, info: (i, 0))],
            out_specs=pl.BlockSpec(memory_space=pl.ANY),
            scratch_shapes=[pltpu.VMEM((MOE_TILE, D_MODEL), F32), pltpu.SemaphoreType.DMA, pltpu.SemaphoreType.DMA]),
        out_shape=jax.ShapeDtypeStruct((n_tiles * MOE_TILE, D_MODEL), F32),
        compiler_params=_cparams("arbitrary"),
        name="moe_dispatch",
    )(info, pos.reshape(steps, 1, 2 * ROW_TILE), x1)


def _expert_kernel(te_ref, info_ref, x_ref, wg_ref, wu_ref, wd_ref, o_ref):
    used = pl.program_id(0) < info_ref[N_EXPERTS]

    @pl.when(used)
    def _():
        x = x_ref[...].astype(BF16)
        g = jnp.dot(x, wg_ref[...].astype(BF16), preferred_element_type=F32)
        u = jnp.dot(x, wu_ref[...].astype(BF16), preferred_element_type=F32)
        h = (g * _sigmoid(g)) * u
        o_ref[...] = jnp.dot(h.astype(BF16), wd_ref[...].astype(BF16), preferred_element_type=F32)

    @pl.when(jnp.logical_not(used))
    def _():
        o_ref[...] = jnp.zeros_like(o_ref)


def _experts(xs, tile_expert, info, w_gate, w_up, w_down, layer):
    rows = xs.shape[0]
    w_gate = w_gate.reshape(DEPTH * N_EXPERTS, D_MODEL, D_EXPERT)
    w_up = w_up.reshape(DEPTH * N_EXPERTS, D_MODEL, D_EXPERT)
    w_down = w_down.reshape(DEPTH * N_EXPERTS, D_EXPERT, D_MODEL)
    expert = lambda i, te, info: (layer * N_EXPERTS + te[i], 0, 0)
    return pl.pallas_call(
        _expert_kernel,
        grid_spec=pltpu.PrefetchScalarGridSpec(
            num_scalar_prefetch=2,
            grid=(rows // MOE_TILE,),
            in_specs=[pl.BlockSpec((MOE_TILE, D_MODEL), lambda i, te, info: (i, 0)),
                      pl.BlockSpec((None, D_MODEL, D_EXPERT), expert),
                      pl.BlockSpec((None, D_MODEL, D_EXPERT), expert),
                      pl.BlockSpec((None, D_EXPERT, D_MODEL), expert)],
            out_specs=pl.BlockSpec((MOE_TILE, D_MODEL), lambda i, te, info: (i, 0))),
        out_shape=jax.ShapeDtypeStruct((rows, D_MODEL), F32),
        compiler_params=_cparams("arbitrary"),
        name="experts",
    )(tile_expert, info, xs, w_gate, w_up, w_down)


COMBINE_TILE = 256


def _combine_ln_kernel(pos_ref, pos_next_ref, ys_hbm, x_ref, route_ref, g_ref, b_ref, o_ref, buf_ref, sem):
    tokens = x_ref.shape[0]
    i = pl.program_id(0)
    slot = i % 2

    def row_copy(src_row, dst_row, into):
        return pltpu.make_async_copy(ys_hbm.at[pl.ds(src_row, 1), :],
                                     buf_ref.at[into, pl.ds(dst_row, 1), :], sem.at[into])

    def issue(idx_ref, into):
        def body(t, _):
            row_copy(idx_ref[0, 0, 2 * t], t, into).start()
            row_copy(idx_ref[0, 0, 2 * t + 1], tokens + t, into).start()
            return 0
        lax.fori_loop(0, tokens, body, 0, unroll=4)

    @pl.when(i == 0)
    def _():
        issue(pos_ref, 0)

    @pl.when(i + 1 < pl.num_programs(0))
    def _():
        issue(pos_next_ref, 1 - slot)

    def drain(t, _):
        row_copy(0, 0, slot).wait()
        row_copy(0, 0, slot).wait()
        return 0

    lax.fori_loop(0, tokens, drain, 0, unroll=4)
    route = route_ref[...]
    y = route[:, 2:3] * buf_ref[slot, 0:tokens, :] + route[:, 3:4] * buf_ref[slot, tokens:, :]
    o_ref[...] = _layer_norm(DEEPNORM_ALPHA * x_ref[...] + y, g_ref[...], b_ref[...])


def _combine_ln(ys, pos, x1, route, g, b, layer):
    n = x1.shape[0]
    steps = n // COMBINE_TILE
    vec = pl.BlockSpec((None, 1, D_MODEL), lambda i: (layer, 0, 0))
    idx = lambda f: pl.BlockSpec((1, 1, 2 * COMBINE_TILE), lambda i: (f(i), 0, 0), memory_space=pltpu.SMEM)
    pos = pos.reshape(steps, 1, 2 * COMBINE_TILE)
    return pl.pallas_call(
        _combine_ln_kernel,
        grid=(steps,),
        in_specs=[idx(lambda i: i), idx(lambda i: jnp.minimum(i + 1, steps - 1)),
                  pl.BlockSpec(memory_space=pl.ANY),
                  pl.BlockSpec((COMBINE_TILE, D_MODEL), lambda i: (i, 0)),
                  pl.BlockSpec((COMBINE_TILE, LANES), lambda i: (i, 0)), vec, vec],
        out_specs=pl.BlockSpec((COMBINE_TILE, D_MODEL), lambda i: (i, 0)),
        out_shape=jax.ShapeDtypeStruct((n, D_MODEL), F32),
        scratch_shapes=[pltpu.VMEM((2, 2 * COMBINE_TILE, D_MODEL), F32), pltpu.SemaphoreType.DMA((2,))],
        compiler_params=_cparams("arbitrary"),
        name="moe_combine_ln",
    )(pos, pos, ys, x1, route, g, b)


def _moe(x1, route, p, layer):
    pos, info, tile_expert = _moe_plan(route, x1.shape[0])
    xs = _dispatch(x1, pos, info)
    ys = _experts(xs, tile_expert, info, p["moe_w_gate"], p["moe_w_up"], p["moe_w_down"], layer)
    return _combine_ln(ys, pos, x1, route, p["ln2_g3"], p["ln2_b3"], layer)


def kernel(x_prompt, x_sample, cache_win_k, cache_win_v, state_wkv, state_shift, w_in, w_out, sgu_ln_g, sgu_ln_b, sgu_ws, sgu_bias, rwkv_mu, rwkv_w0, rwkv_w2, rwkv_a0, rwkv_a2, rwkv_g2, rwkv_k_k, rwkv_k_a, rwkv_r_k, rwkv_lnx_g, rwkv_lnx_b, ln1_g, ln1_b, ln2_g, ln2_b, moe_router1, moe_router1_b, moe_router2, moe_router2_b, moe_w_gate, moe_w_up, moe_w_down):
    bp, tp, _ = x_prompt.shape
    bs, ts, _ = x_sample.shape
    n_p, n_s = bp * tp, bs * ts
    depth = w_in.shape[0]
    assert depth == DEPTH and tp <= DILATED_PATTERNS[-1][0] and tp % CHUNK == 0
    assert n_p % ROW_TILE == 0 and n_s % ROW_TILE == 0 and n_p % n_s == 0

    per_row = lambda a: a[:, None, :]
    p = dict(rwkv_mu=rwkv_mu, rwkv_w0=rwkv_w0, rwkv_a0=rwkv_a0, rwkv_k_k=rwkv_k_k, rwkv_k_a=rwkv_k_a,
             rwkv_r_k=rwkv_r_k, rwkv_lnx_g=rwkv_lnx_g, rwkv_lnx_b=rwkv_lnx_b,
             moe_w_gate=moe_w_gate, moe_w_up=moe_w_up, moe_w_down=moe_w_down,
             ln2_g3=per_row(ln2_g), ln2_b3=per_row(ln2_b))
    w_in_b = w_in.astype(BF16)
    w_kvt_b = jnp.swapaxes(w_in[:, :, D_A:3 * D_A], 1, 2).astype(BF16)
    w_out_b = w_out.astype(BF16)
    cache_kt, cache_vt = _window_transposed(cache_win_k), _window_transposed(cache_win_v)
    lora = _rwkv_lora_weights(rwkv_w2, rwkv_a2, rwkv_g2)
    wr, br = _router_weights(moe_router1, moe_router1_b, moe_router2, moe_router2_b)
    sgu_bias_full = _expand_sgu_bias(sgu_bias)
    sgu_coef = _sgu_sample_coef(sgu_ws, sgu_bias, ts)
    sgu_g, sgu_b = per_row(sgu_ln_g), per_row(sgu_ln_b)
    shift0 = state_shift[:, :, None, :]

    x = jnp.concatenate([x_prompt.reshape(n_p, D_MODEL), x_sample.reshape(n_s, D_MODEL)], axis=0)
    outs = [[] for _ in range(9)]
    for layer in range(depth):
        q_p, kt_p, vt_p, uv_p, pc_p = _in_proj_prompt(x, bp, tp, w_in_b, w_kvt_b, layer)
        qkv_s, uv_s, pc_s = _in_proj_sample(x, n_p, n_s, w_in_b, layer)
        vec = _rwkv_vec_table(p, layer)

        oa_p = _attn_prompt(q_p, kt_p, vt_p, bp, tp)
        ob_p = _sgu_prompt(uv_p, n_p, sgu_ws, sgu_bias_full, sgu_g, sgu_b, layer)
        oc_p, wkv_p = _rwkv_prompt(pc_p, bp, tp, lora[layer], vec)

        pc_s = pc_s.reshape(bs, ts, D_C_PROJ)
        oa_s = _attn_sample(qkv_s, cache_kt, cache_vt, layer, bs, ts)
        ob_s, vn_s = _sgu_sample(uv_s, 0, n_s, sgu_coef, sgu_g, sgu_b, layer)
        oc_s, wkv_s = _rwkv_sample(jnp.pad(pc_s, ((0, 0), (0, 8 - ts), (0, 0))), shift0, state_wkv,
                                   lora[layer], vec, layer, ts)
        oc_s = oc_s[:, :ts].reshape(n_s, D_C)

        x1, route = _out_proj((oa_p, ob_p, oc_p), (oa_s, ob_s, oc_s), x, w_out_b,
                              per_row(ln1_g), per_row(ln1_b), wr, br, layer)
        x = _moe(x1, route, p, layer)

        heads_p = lambda t: jnp.transpose(t.reshape(bp, N_HEADS_A, HEAD_DIM, tp), (0, 3, 1, 2))
        heads_s = lambda t: t.reshape(bs, ts, N_HEADS_A, HEAD_DIM)
        layer_outs = (heads_p(kt_p), heads_p(vt_p), wkv_p,
                      pc_p.reshape(bp, tp, D_C_PROJ)[:, -1],
                      heads_s(qkv_s[:, D_A:2 * D_A]), heads_s(qkv_s[:, 2 * D_A:]), wkv_s, pc_s[:, -1],
                      vn_s.reshape(bs, ts, D_B))
        for acc, o in zip(outs, layer_outs):
            acc.append(o)

    return (x[:n_p].reshape(bp, tp, D_MODEL), x[n_p:].reshape(bs, ts, D_MODEL)) + tuple(jnp.stack(o, 0) for o in outs)
```

```python
import functools
import math

import numpy as np
import jax
import jax.numpy as jnp
from jax import lax
from jax.experimental import pallas as pl
from jax.experimental.pallas import tpu as pltpu

F32 = jnp.float32
BF16 = jnp.bfloat16

D_MODEL = 1024
HEAD_DIM = 64
N_HEADS_A = 8
N_HEADS_B = 4
N_HEADS_C = 4
D_A = N_HEADS_A * HEAD_DIM
D_B = N_HEADS_B * HEAD_DIM
D_C = N_HEADS_C * HEAD_DIM
DILATED_PATTERNS = ((128, 1), (512, 4), (2048, 16))
CHUNK = 128
LORA_W, LORA_A, LORA_G = 32, 32, 64
D_LORA = LORA_W + LORA_A + LORA_G
D_C_PROJ = 3 * D_C + D_LORA
D_QKV = 3 * D_A
D_UV = 2 * D_B
D_IN = D_QKV + D_UV + D_C_PROJ
N_GROUPS = 4
EXPERTS_PER_GROUP = 8
N_EXPERTS = N_GROUPS * EXPERTS_PER_GROUP
D_EXPERT = 512
DEPTH = 2
DEEPNORM_ALPHA = (2 * DEPTH) ** 0.25
LN_EPS = 1e-5
GN_EPS = 64e-5
DECAY_SCALE = math.exp(-0.5)
L2_EPS = 1e-12
NEG = -1e30

LANES = 128
ROW_TILE = 512
Q_BLOCK = 128
RWKV_CHUNK = 64
MOE_TILE = 256
VMEM_LIMIT = 56 * 1024 * 1024


def _cparams(*sem):
    return pltpu.CompilerParams(dimension_semantics=sem, vmem_limit_bytes=VMEM_LIMIT)


def _dg(a, b, ca=1, cb=0):
    return lax.dot_general(a, b, (((ca,), (cb,)), ((), ())), preferred_element_type=F32)


def _split(x):
    hi = x.astype(BF16)
    lo = (x - hi.astype(F32)).astype(BF16)
    return hi, lo


def _dot3(a, b, ca=1, cb=0):
    ah, al = _split(a)
    bh, bl = _split(b)
    return _dg(ah, bh, ca, cb) + (_dg(ah, bl, ca, cb) + _dg(al, bh, ca, cb))


def _dot_exact_lhs(a_bf16, b, ca=1, cb=0):
    bh, bl = _split(b)
    return _dg(a_bf16, bh, ca, cb) + _dg(a_bf16, bl, ca, cb)


def _dot_exact_rhs(a, b_bf16, ca=1, cb=0):
    ah, al = _split(a)
    return _dg(ah, b_bf16, ca, cb) + _dg(al, b_bf16, ca, cb)


def _sigmoid(x):
    return 1.0 / (1.0 + jnp.exp(-x))


def _gelu(x):
    return 0.5 * x * (1.0 + lax.erf(x * (2.0 ** -0.5)))


def _layer_norm(z, g, b):
    mu = jnp.mean(z, axis=-1, keepdims=True)
    d = z - mu
    var = jnp.mean(d * d, axis=-1, keepdims=True)
    return d * lax.rsqrt(var + LN_EPS) * g + b


def _in_proj_sample_kernel(x_ref, w_ref, qkv_ref, uv_ref, pc_ref):
    x = x_ref[...].astype(BF16)
    qkv_ref[...] = jnp.dot(x, w_ref[:, 0:D_QKV], preferred_element_type=F32)
    uv_ref[...] = jnp.dot(x, w_ref[:, D_QKV:D_QKV + D_UV], preferred_element_type=F32)
    pc_ref[...] = jnp.dot(x, w_ref[:, D_QKV + D_UV:D_IN], preferred_element_type=F32)


def _in_proj_sample(x, first_row, n, w_in_bf16, layer):
    blk = first_row // n
    return pl.pallas_call(
        _in_proj_sample_kernel,
        grid=(1,),
        in_specs=[pl.BlockSpec((n, D_MODEL), lambda i: (blk, 0)),
                  pl.BlockSpec((None, D_MODEL, D_IN), lambda i: (layer, 0, 0))],
        out_specs=[pl.BlockSpec((n, w), lambda i: (0, 0)) for w in (D_QKV, D_UV, D_C_PROJ)],
        out_shape=[jax.ShapeDtypeStruct((n, w), F32) for w in (D_QKV, D_UV, D_C_PROJ)],
        compiler_params=_cparams("arbitrary"),
        name="in_proj_sample",
    )(x, w_in_bf16)


def _in_proj_prompt_kernel(x_ref, w_ref, wkvt_ref, q_ref, kt_ref, vt_ref, uv_ref, pc_ref):
    x = x_ref[...].astype(BF16)
    q_ref[...] = jnp.dot(x, w_ref[:, 0:D_A], preferred_element_type=F32)
    kt_ref[...] = _dg(wkvt_ref[0:D_A, :], x, 1, 1)
    vt_ref[...] = _dg(wkvt_ref[D_A:, :], x, 1, 1)
    uv_ref[...] = jnp.dot(x, w_ref[:, D_QKV:D_QKV + D_UV], preferred_element_type=F32)
    pc_ref[...] = jnp.dot(x, w_ref[:, D_QKV + D_UV:D_IN], preferred_element_type=F32)


def _in_proj_prompt(x, batch, seq, w_in_bf16, w_kvt_bf16, layer):
    n = batch * seq
    per_seq = seq // ROW_TILE
    row = lambda w: pl.BlockSpec((ROW_TILE, w), lambda i: (i, 0))
    tr = pl.BlockSpec((None, D_A, ROW_TILE), lambda i: (i // per_seq, 0, i % per_seq))
    return pl.pallas_call(
        _in_proj_prompt_kernel,
        grid=(n // ROW_TILE,),
        in_specs=[row(D_MODEL), pl.BlockSpec((None, D_MODEL, D_IN), lambda i: (layer, 0, 0)),
                  pl.BlockSpec((None, 2 * D_A, D_MODEL), lambda i: (layer, 0, 0))],
        out_specs=[row(D_A), tr, tr, row(D_UV), row(D_C_PROJ)],
        out_shape=[jax.ShapeDtypeStruct((n, D_A), F32), jax.ShapeDtypeStruct((batch, D_A, seq), F32),
                   jax.ShapeDtypeStruct((batch, D_A, seq), F32), jax.ShapeDtypeStruct((n, D_UV), F32),
                   jax.ShapeDtypeStruct((n, D_C_PROJ), F32)],
        compiler_params=_cparams("parallel"),
        name="in_proj_prompt",
    )(x, w_in_bf16, w_kvt_bf16)


def _pattern_count(delta):
    c = np.zeros(delta.shape, np.float64)
    for window, dilation in DILATED_PATTERNS:
        c += (delta >= 0) & (delta <= window) & (delta % dilation == 0)
    return c


def _log_count(delta):
    c = _pattern_count(delta)
    return np.where(c > 0, np.log(np.maximum(c, 1.0)), NEG).astype(np.float32)


def _prompt_bias_table(seq):
    i = np.arange(Q_BLOCK)[:, None]
    n = np.arange(seq)[None, :]
    tab = _log_count(seq - Q_BLOCK + i - n)
    return np.concatenate([tab, tab], axis=0)


def _attn_prompt_kernel(q_ref, kt_ref, vt_ref, bias_ref, o_ref, kb_ref, vb_ref, *, nb):
    seq = nb * Q_BLOCK
    kb_ref[...] = kt_ref[...].astype(BF16)
    vb_ref[...] = vt_ref[...].astype(BF16)
    first = lax.broadcasted_iota(jnp.int32, (Q_BLOCK, LANES), 1) < HEAD_DIM
    scale = HEAD_DIM ** -0.5
    for qb in range(nb):
        keys = (qb + 1) * Q_BLOCK
        q = q_ref[qb * Q_BLOCK:(qb + 1) * Q_BLOCK, :] * scale
        q2 = jnp.concatenate([jnp.where(first, q, 0.0), jnp.where(first, 0.0, q)], axis=0).astype(BF16)
        s = jnp.dot(q2, kb_ref[:, 0:keys], preferred_element_type=F32) + bias_ref[:, seq - keys:seq]
        p = jnp.exp(s - jnp.max(s, axis=-1, keepdims=True))
        denom = jnp.sum(p, axis=-1, keepdims=True)
        o = _dg(p.astype(BF16), vb_ref[:, 0:keys], 1, 1) / denom
        o_ref[qb * Q_BLOCK:(qb + 1) * Q_BLOCK, :] = jnp.where(first, o[:Q_BLOCK], o[Q_BLOCK:])


def _attn_prompt(q, kt, vt, batch, seq):
    nb = seq // Q_BLOCK
    pairs = D_A // LANES
    bias = jnp.asarray(_prompt_bias_table(seq))
    tr = pl.BlockSpec((None, LANES, seq), lambda b, p: (b, p, 0))
    return pl.pallas_call(
        functools.partial(_attn_prompt_kernel, nb=nb),
        grid=(batch, pairs),
        in_specs=[pl.BlockSpec((seq, LANES), lambda b, p: (b, p)), tr, tr,
                  pl.BlockSpec((2 * Q_BLOCK, seq), lambda b, p: (0, 0))],
        out_specs=pl.BlockSpec((seq, LANES), lambda b, p: (b, p)),
        out_shape=jax.ShapeDtypeStruct((batch * seq, D_A), F32),
        scratch_shapes=[pltpu.VMEM((LANES, seq), BF16), pltpu.VMEM((LANES, seq), BF16)],
        compiler_params=_cparams("parallel", "parallel"),
        name="attn_prompt",
    )(q, kt, vt, bias)


def _sgu_prompt_kernel(u_ref, v_ref, ws_ref, bias_ref, g_ref, b_ref, o_ref):
    u = _gelu(u_ref[...])
    vn = _layer_norm(_gelu(v_ref[...]), g_ref[...], b_ref[...]).astype(BF16)
    r = lax.broadcasted_iota(jnp.int32, (CHUNK, CHUNK), 0)
    c = lax.broadcasted_iota(jnp.int32, (CHUNK, CHUNK), 1)
    head = lax.broadcasted_iota(jnp.int32, (CHUNK, D_B), 1) // HEAD_DIM
    wm = [jnp.where(r >= c, ws_ref[h], 0.0).astype(BF16) for h in range(N_HEADS_B)]
    for j in range(u_ref.shape[0] // CHUNK):
        rows = slice(j * CHUNK, (j + 1) * CHUNK)
        z = bias_ref[...]
        for h in range(N_HEADS_B):
            z = z + jnp.where(head == h, jnp.dot(wm[h], vn[rows], preferred_element_type=F32), 0.0)
        o_ref[rows, :] = u[rows] * z


def _expand_sgu_bias(bias):
    return jnp.repeat(jnp.swapaxes(bias, 1, 2), HEAD_DIM, axis=2)


def _sgu_prompt(uv, n, ws, bias_full, g, b, layer):
    vec = pl.BlockSpec((None, 1, D_B), lambda i: (layer, 0, 0))
    rows = math.gcd(n, ROW_TILE)
    return pl.pallas_call(
        _sgu_prompt_kernel,
        grid=(n // rows,),
        in_specs=[pl.BlockSpec((rows, D_B), lambda i: (i, 0)), pl.BlockSpec((rows, D_B), lambda i: (i, 1)),
                  pl.BlockSpec((None, N_HEADS_B, CHUNK, CHUNK), lambda i: (layer, 0, 0, 0)),
                  pl.BlockSpec((None, CHUNK, D_B), lambda i: (layer, 0, 0)), vec, vec],
        out_specs=pl.BlockSpec((rows, D_B), lambda i: (i, 0)),
        out_shape=jax.ShapeDtypeStruct((n, D_B), F32),
        compiler_params=_cparams("parallel"),
        name="sgu_prompt",
    )(uv, uv, ws, bias_full, g, b)


def _rwkv_lora_weights(w2, a2, g2):
    depth = w2.shape[0]
    w = jnp.zeros((depth, D_LORA, 3 * D_C), F32)
    w = w.at[:, 0:LORA_W, 0:D_C].set(w2)
    w = w.at[:, LORA_W:LORA_W + LORA_A, D_C:2 * D_C].set(a2)
    return w.at[:, LORA_W + LORA_A:, 2 * D_C:].set(g2)


def _rwkv_features(xs, wl, vec):
    r = xs[:, 0:D_C]
    k = xs[:, D_C:2 * D_C]
    v = xs[:, 2 * D_C:3 * D_C]
    tail = xs[:, 3 * D_C:]
    lane = lax.broadcasted_iota(jnp.int32, tail.shape, 1)
    feats = jnp.where(lane < LORA_W, jnp.tanh(tail), jnp.where(lane < LORA_W + LORA_A, tail, _sigmoid(tail)))
    lora = _dot3(feats, wl)
    logw = -DECAY_SCALE * _sigmoid(vec["w0"] + lora[:, 0:D_C])
    a = _sigmoid(vec["a0"] + lora[:, D_C:2 * D_C])
    gate = lora[:, 2 * D_C:]
    kk = k * vec["k_k"]
    ri = lax.broadcasted_iota(jnp.int32, (D_C, D_C), 0) // HEAD_DIM
    ci = lax.broadcasted_iota(jnp.int32, (D_C, D_C), 1) // HEAD_DIM
    head_ones = jnp.where(ri == ci, 1.0, 0.0).astype(BF16)
    kk = kk / jnp.maximum(jnp.sqrt(_dot_exact_rhs(kk * kk, head_ones)), L2_EPS)
    k = k * (1.0 + (a - 1.0) * vec["k_a"])
    bonus = _dot_exact_rhs(r * k * vec["r_k"], head_ones) * v
    return r, k, v, logw, kk, kk * a, gate, bonus


def _rwkv_chunk_fn(E, C, n_valid, wl_ref, vec_ref):
    H = N_HEADS_C
    HC = H * C
    n = E * HC
    names = ("mu_r", "mu_k", "mu_v", "w0", "a0", "k_k", "k_a", "r_k", "lnx_g", "lnx_b")
    vec = {nm: vec_ref[i:i + 1, :] for i, nm in enumerate(names)}
    mu = jnp.concatenate([vec["mu_r"], vec["mu_k"], vec["mu_v"], vec_ref[len(names):len(names) + 1, 0:D_LORA]], axis=1)

    first_row = lax.broadcasted_iota(jnp.int32, (E * C, D_C_PROJ), 0) % C == 0
    is_token = lax.broadcasted_iota(jnp.int32, (E * C, D_C), 0) % C < n_valid
    srow = lax.broadcasted_iota(jnp.int32, (n, n), 0)
    scol = lax.broadcasted_iota(jnp.int32, (n, n), 1)
    same_block = (srow // C) == (scol // C)
    strict = same_block & ((srow % C) > (scol % C))
    incl = same_block & ((srow % C) >= (scol % C))
    eye = jnp.where(srow == scol, 1.0, 0.0)
    own = (lax.broadcasted_iota(jnp.int32, (1, H, 1, D_C), 1)
           == lax.broadcasted_iota(jnp.int32, (1, H, 1, D_C), 3) // HEAD_DIM)
    own_rows = jnp.broadcast_to(own, (E, H, C, D_C)).reshape(n, D_C)
    tr = lax.broadcasted_iota(jnp.int32, (E * C, E * C), 0)
    tc = lax.broadcasted_iota(jnp.int32, (E * C, E * C), 1)
    cum = jnp.where(((tr // C) == (tc // C)) & (tr >= tc), 1.0, 0.0).astype(BF16)

    def stack(x):
        x4 = jnp.broadcast_to(x.reshape(E, 1, C, D_C), (E, H, C, D_C))
        return jnp.where(own, x4, 0.0).reshape(n, D_C)

    def unstack(xs):
        x4 = xs.reshape(E, H, C, D_C)
        return ((x4[:, 0] + x4[:, 1]) + (x4[:, 2] + x4[:, 3])).reshape(E * C, D_C)

    def per_sequence(x, f):
        return jnp.broadcast_to(f(x.reshape(E, C, D_C)), (E, C, D_C)).reshape(E * C, D_C)

    def chunk(pc, prev_rows, s0):
        prev0 = jnp.broadcast_to(prev_rows, (E, C, D_C_PROJ)).reshape(E * C, D_C_PROJ)
        prev = jnp.where(first_row, prev0, pltpu.roll(pc, 1, axis=0))
        xs = pc + (prev - pc) * mu
        r, k, v, logw, kk, b, gate, bonus = _rwkv_features(xs, wl_ref[...], vec)
        if n_valid < C:
            logw = jnp.where(is_token, logw, 0.0)
            kk = jnp.where(is_token, kk, 0.0)
            b = jnp.where(is_token, b, 0.0)
            k = jnp.where(is_token, k, 0.0)

        lc = _dot_exact_lhs(cum, logw)
        lc_end = per_sequence(lc, lambda t: t[:, C - 1:C, :])
        w_inv = jnp.exp(-lc)
        a_t = stack(-kk * jnp.exp(lc - logw))
        r_t = stack(r * jnp.exp(lc))
        b_t = stack(b * w_inv)
        k_t = stack(k * w_inv)
        to_end = jnp.exp(lc_end - lc)
        b_e = stack(b * to_end)
        k_e = stack(k * to_end)
        v_s = stack(v)

        ar = jnp.concatenate([a_t, r_t], axis=0).astype(BF16)
        gram = _dg(ar, jnp.concatenate([b_t, k_t], axis=0).astype(BF16), 1, 1)
        ab = jnp.where(strict, gram[:n, :n], 0.0)
        ak = jnp.where(strict, gram[:n, n:], 0.0).astype(BF16)
        rb = jnp.where(incl, gram[n:, :n], 0.0).astype(BF16)
        rk = jnp.where(incl, gram[n:, n:], 0.0).astype(BF16)

        inv = eye + ab
        power = ab.astype(BF16)
        for step in range(int(math.log2(C)) - 1):
            power = _dg(power, power)
            inv = inv + _dg(power.astype(BF16), inv.astype(BF16))
            power = power.astype(BF16)

        seq_rows = lambda x, e, off=0: x[off + e * HC:off + (e + 1) * HC]
        from_s0 = [_dg(jnp.concatenate([seq_rows(ar, e), seq_rows(ar, e, n)], axis=0), s0[e].astype(BF16), 1, 1)
                   for e in range(E)]
        a_s0 = jnp.concatenate([f[:HC] for f in from_s0], axis=0)
        r_s0 = jnp.concatenate([f[HC:] for f in from_s0], axis=0)
        v_b = v_s.astype(BF16)
        u_b = _dg(inv.astype(BF16), (a_s0 + _dg(ak, v_b)).astype(BF16)).astype(BF16)
        y = unstack(r_s0 + _dg(rb, u_b) + _dg(rk, v_b))
        b_e = b_e.astype(BF16)
        k_e = k_e.astype(BF16)
        s_end = [s0[e] * jnp.exp(lc_end[e * C:e * C + 1]) + _dg(seq_rows(u_b, e), seq_rows(b_e, e), 0, 0)
                 + _dg(seq_rows(v_b, e), seq_rows(k_e, e), 0, 0) for e in range(E)]

        ys = stack(y)
        mean = jnp.sum(ys, axis=-1, keepdims=True) * (1.0 / HEAD_DIM)
        d = jnp.where(own_rows, ys - mean, 0.0)
        var = jnp.sum(d * d, axis=-1, keepdims=True) * (1.0 / HEAD_DIM)
        yn = unstack(d * lax.rsqrt(var + GN_EPS)) * vec["lnx_g"] + vec["lnx_b"]
        return (yn + bonus) * gate, s_end

    return chunk


def _store_head_states(s_out_ref, s):
    for h in range(N_HEADS_C):
        s_out_ref[h] = s[h * HEAD_DIM:(h + 1) * HEAD_DIM, h * HEAD_DIM:(h + 1) * HEAD_DIM]


def _rwkv_prompt_kernel(pc_ref, wl_ref, vec_ref, o_ref, s_out_ref, s_ref, prev_ref):
    C = RWKV_CHUNK
    group, span, _ = pc_ref.shape
    chunk = _rwkv_chunk_fn(1, C, C, wl_ref, vec_ref)

    @pl.when(pl.program_id(1) == 0)
    def _():
        s_ref[...] = jnp.zeros_like(s_ref)
        prev_ref[...] = jnp.zeros_like(prev_ref)

    def step(c, _):
        t0 = pl.multiple_of(c * C, C)
        for g in range(group):
            pc = pc_ref[g, pl.ds(t0, C), :]
            out, (s_end,) = chunk(pc, prev_ref[g][None], [s_ref[g]])
            prev_ref[g] = pc[C - 1:C, :]
            s_ref[g] = s_end
            o_ref[g, pl.ds(t0, C), :] = out
        return 0

    lax.fori_loop(0, span // C, step, 0)

    @pl.when(pl.program_id(1) == pl.num_programs(1) - 1)
    def _():
        for g in range(group):
            _store_head_states(s_out_ref.at[g], s_ref[g])


def _rwkv_sample_kernel(pc_ref, shift_ref, s_in_ref, wl_ref, vec_ref, o_ref, s_out_ref, *, n_valid):
    group, C, _ = pc_ref.shape
    zero = jnp.zeros((HEAD_DIM, HEAD_DIM), F32)
    s0 = [jnp.concatenate(
        [jnp.concatenate([s_in_ref[e, h] if g == h else zero for g in range(N_HEADS_C)], axis=1)
         for h in range(N_HEADS_C)], axis=0) for e in range(group)]
    chunk = _rwkv_chunk_fn(group, C, n_valid, wl_ref, vec_ref)
    out, s_end = chunk(pc_ref[...].reshape(group * C, D_C_PROJ), shift_ref[...], s0)
    o_ref[...] = out.reshape(group, C, D_C)
    for e in range(group):
        _store_head_states(s_out_ref.at[e], s_end[e])


RWKV_SAMPLE_GROUP = 8


def _rwkv_sample(pc_pad, shift0, wkv0, wl, vec, layer, n_valid):
    batch, rows, _ = pc_pad.shape
    group = math.gcd(batch, RWKV_SAMPLE_GROUP)
    state = pl.BlockSpec((None, group, N_HEADS_C, HEAD_DIM, HEAD_DIM), lambda b: (layer, b, 0, 0, 0))
    return pl.pallas_call(
        functools.partial(_rwkv_sample_kernel, n_valid=n_valid),
        grid=(batch // group,),
        in_specs=[pl.BlockSpec((group, rows, D_C_PROJ), lambda b: (b, 0, 0)),
                  pl.BlockSpec((None, group, 1, D_C_PROJ), lambda b: (layer, b, 0, 0)),
                  state,
                  pl.BlockSpec((D_LORA, 3 * D_C), lambda b: (0, 0)),
                  pl.BlockSpec((16, D_C), lambda b: (0, 0))],
        out_specs=[pl.BlockSpec((group, rows, D_C), lambda b: (b, 0, 0)),
                   pl.BlockSpec((group, N_HEADS_C, HEAD_DIM, HEAD_DIM), lambda b: (b, 0, 0, 0))],
        out_shape=[jax.ShapeDtypeStruct((batch, rows, D_C), F32),
                   jax.ShapeDtypeStruct((batch, N_HEADS_C, HEAD_DIM, HEAD_DIM), F32)],
        compiler_params=_cparams("parallel"),
        name="rwkv_sample",
    )(pc_pad, shift0, wkv0, wl, vec)


def _rwkv_vec_table(p, layer):
    mu = p["rwkv_mu"][layer]
    rows = [mu[0:D_C], mu[D_C:2 * D_C], mu[2 * D_C:3 * D_C], p["rwkv_w0"][layer], p["rwkv_a0"][layer],
            p["rwkv_k_k"][layer], p["rwkv_k_a"][layer], p["rwkv_r_k"][layer].reshape(D_C),
            p["rwkv_lnx_g"][layer], p["rwkv_lnx_b"][layer], jnp.pad(mu[3 * D_C:], (0, D_C - D_LORA))]
    rows += [jnp.zeros((D_C,), F32)] * (16 - len(rows))
    return jnp.stack(rows, 0)


RWKV_GROUP = 8
RWKV_SPAN = 256


def _rwkv_prompt(pc, batch, seq, wl, vec):
    group = math.gcd(batch, RWKV_GROUP)
    span = math.gcd(seq, RWKV_SPAN)
    o, s = pl.pallas_call(
        _rwkv_prompt_kernel,
        grid=(batch // group, seq // span),
        in_specs=[pl.BlockSpec((group, span, D_C_PROJ), lambda b, t: (b, t, 0)),
                  pl.BlockSpec((D_LORA, 3 * D_C), lambda b, t: (0, 0)),
                  pl.BlockSpec((16, D_C), lambda b, t: (0, 0))],
        out_specs=[pl.BlockSpec((group, span, D_C), lambda b, t: (b, t, 0)),
                   pl.BlockSpec((group, N_HEADS_C, HEAD_DIM, HEAD_DIM), lambda b, t: (b, 0, 0, 0))],
        out_shape=[jax.ShapeDtypeStruct((batch, seq, D_C), F32),
                   jax.ShapeDtypeStruct((batch, N_HEADS_C, HEAD_DIM, HEAD_DIM), F32)],
        scratch_shapes=[pltpu.VMEM((group, D_C, D_C), F32), pltpu.VMEM((group, 1, D_C_PROJ), F32)],
        compiler_params=_cparams("parallel", "arbitrary"),
        name="rwkv_prompt",
    )(pc[:batch * seq].reshape(batch, seq, D_C_PROJ), wl, vec)
    return o.reshape(batch * seq, D_C), s


NEW_PAD = 8


def _sample_bias_tables(w_buf, dec_seq):
    q_pos = w_buf + np.repeat(np.arange(dec_seq), N_HEADS_A)[:, None]
    window = _log_count(q_pos - np.arange(w_buf)[None, :])
    new = _log_count(q_pos - (w_buf + np.arange(NEW_PAD))[None, :])
    new[:, dec_seq:] = NEG
    return window, new


def _attn_sample_kernel(q_ref, knew_ref, vnew_ref, kt_ref, vt_ref, bias_w_ref, bias_n_ref, o_ref):
    rows = q_ref.shape[0]
    dec_seq = rows // N_HEADS_A
    q = q_ref[...].astype(BF16)
    s_w = jnp.dot(q, kt_ref[...].astype(BF16), preferred_element_type=F32) + bias_w_ref[...]
    s_n = _dg(q, knew_ref[...].astype(BF16), 1, 1) + bias_n_ref[...]
    m = jnp.maximum(jnp.max(s_w, axis=-1, keepdims=True), jnp.max(s_n, axis=-1, keepdims=True))
    p_w = jnp.exp(s_w - m)
    p_n = jnp.exp(s_n - m)
    denom = jnp.sum(p_w, axis=-1, keepdims=True) + jnp.sum(p_n, axis=-1, keepdims=True)
    o = (_dg(p_w.astype(BF16), vt_ref[...].astype(BF16), 1, 1)
         + jnp.dot(p_n.astype(BF16), vnew_ref[...].astype(BF16), preferred_element_type=F32)) / denom
    head = lax.broadcasted_iota(jnp.int32, (N_HEADS_A, D_A), 0)
    col_head = lax.broadcasted_iota(jnp.int32, (N_HEADS_A, D_A), 1) // HEAD_DIM
    for i in range(dec_seq):
        tile = o[i * N_HEADS_A:(i + 1) * N_HEADS_A, :]
        o_ref[i:i + 1, :] = jnp.sum(jnp.where(head == col_head, tile, 0.0), axis=0, keepdims=True)


def _attn_sample(qkv_s, cache_kt, cache_vt, layer, batch, dec_seq):
    w_buf = cache_kt.shape[3]
    assert dec_seq <= NEW_PAD
    rows = dec_seq * N_HEADS_A
    q = qkv_s[:, 0:D_A].reshape(batch, dec_seq, N_HEADS_A, 1, HEAD_DIM) * HEAD_DIM ** -0.5
    q_bd = (q * jnp.eye(N_HEADS_A, dtype=F32)[None, None, :, :, None]).reshape(batch, rows, D_A)
    pad_new = lambda t: jnp.pad(t.reshape(batch, dec_seq, D_A), ((0, 0), (0, NEW_PAD - dec_seq), (0, 0)))
    bias_w, bias_n = (jnp.asarray(t) for t in _sample_bias_tables(w_buf, dec_seq))
    cache = pl.BlockSpec((None, None, D_A, w_buf), lambda b: (layer, b, 0, 0))
    new = pl.BlockSpec((None, NEW_PAD, D_A), lambda b: (b, 0, 0))
    o = pl.pallas_call(
        _attn_sample_kernel,
        grid=(batch,),
        in_specs=[pl.BlockSpec((None, rows, D_A), lambda b: (b, 0, 0)), new, new, cache, cache,
                  pl.BlockSpec(bias_w.shape, lambda b: (0, 0)), pl.BlockSpec(bias_n.shape, lambda b: (0, 0))],
        out_specs=pl.BlockSpec((None, dec_seq, D_A), lambda b: (b, 0, 0)),
        out_shape=jax.ShapeDtypeStruct((batch, dec_seq, D_A), F32),
        compiler_params=_cparams("parallel"),
        name="attn_sample",
    )(q_bd, pad_new(qkv_s[:, D_A:2 * D_A]), pad_new(qkv_s[:, 2 * D_A:]), cache_kt, cache_vt, bias_w, bias_n)
    return o.reshape(batch * dec_seq, D_A)


def _window_transposed(cache):
    depth, batch, w_buf = cache.shape[:3]
    return jnp.transpose(cache, (0, 1, 3, 4, 2)).reshape(depth, batch, D_A, w_buf)


def _sgu_sample_coef(ws, bias, dec_seq):
    assert 8 % dec_seq == 0
    t = np.arange(8) % dec_seq
    out = []
    for k in range(dec_seq):
        w = ws[:, :, t, np.maximum(t - k, 0)] * jnp.asarray(t >= k, F32)
        out.append(w)
    out.append(bias[:, :, t])
    coef = jnp.stack(out, axis=1)
    return jnp.repeat(jnp.swapaxes(coef, 2, 3), HEAD_DIM, axis=3)


def _sgu_sample_kernel(u_ref, v_ref, coef_ref, g_ref, b_ref, o_ref, vn_ref):
    n, width = u_ref.shape
    taps = coef_ref.shape[0] - 1
    u = _gelu(u_ref[...])
    vn = _layer_norm(_gelu(v_ref[...]), g_ref[...], b_ref[...])
    vn_ref[...] = vn
    z = jnp.broadcast_to(coef_ref[taps][None], (n // 8, 8, width))
    for k in range(taps):
        shifted = vn if k == 0 else pltpu.roll(vn, k, axis=0)
        z = z + coef_ref[k][None] * shifted.reshape(n // 8, 8, width)
    o_ref[...] = u * z.reshape(n, width)


def _sgu_sample(uv, first_row, n, coef, g, b, layer):
    blk = first_row // n
    vec = pl.BlockSpec((None, 1, D_B), lambda i: (layer, 0, 0))
    return pl.pallas_call(
        _sgu_sample_kernel,
        grid=(1,),
        in_specs=[pl.BlockSpec((n, D_B), lambda i: (blk, 0)), pl.BlockSpec((n, D_B), lambda i: (blk, 1)),
                  pl.BlockSpec((None,) + coef.shape[1:], lambda i: (layer, 0, 0, 0)), vec, vec],
        out_specs=[pl.BlockSpec((n, D_B), lambda i: (0, 0)), pl.BlockSpec((n, D_B), lambda i: (0, 0))],
        out_shape=[jax.ShapeDtypeStruct((n, D_B), F32), jax.ShapeDtypeStruct((n, D_B), F32)],
        compiler_params=_cparams("arbitrary"),
        name="sgu_sample",
    )(uv, uv, coef, g, b)


def _router_weights(r1, r1b, r2, r2b):
    depth = r1.shape[0]
    w = jnp.concatenate([r1, r2, jnp.zeros((depth, D_MODEL, LANES - N_GROUPS - N_EXPERTS), F32)], axis=-1)
    b = jnp.concatenate([r1b, r2b.reshape(depth, N_EXPERTS), jnp.zeros((depth, LANES - N_GROUPS - N_EXPERTS), F32)], axis=-1)
    return w, b[:, None, :]


def _route(x, wr, br):
    logits = _dot3(x, wr) + br
    lane_i = lax.broadcasted_iota(jnp.int32, logits.shape, 1)
    lane = lane_i.astype(F32)
    far = float(LANES)
    is_g = lane_i < N_GROUPS
    lg = jnp.where(is_g, logits, NEG)
    gmax = jnp.max(lg, axis=-1, keepdims=True)
    grp = jnp.min(jnp.where(lg == gmax, lane, far), axis=-1, keepdims=True)
    gate = 1.0 / jnp.sum(jnp.where(is_g, jnp.exp(lg - gmax), 0.0), axis=-1, keepdims=True)
    lo = N_GROUPS + EXPERTS_PER_GROUP * grp
    le = jnp.where((lane >= lo) & (lane < lo + EXPERTS_PER_GROUP), logits, NEG)
    t1 = jnp.max(le, axis=-1, keepdims=True)
    i1 = jnp.min(jnp.where(le == t1, lane, far), axis=-1, keepdims=True)
    le2 = jnp.where(lane == i1, NEG, le)
    t2 = jnp.max(le2, axis=-1, keepdims=True)
    i2 = jnp.min(jnp.where(le2 == t2, lane, far), axis=-1, keepdims=True)
    e = jnp.exp(t2 - t1)
    w1 = gate / (1.0 + e)
    w2 = gate * e / (1.0 + e)
    return jnp.where(lane_i == 0, i1 - N_GROUPS, jnp.where(lane_i == 1, i2 - N_GROUPS,
           jnp.where(lane_i == 2, w1, jnp.where(lane_i == 3, w2, 0.0))))


def _out_proj_kernel(oa_p_ref, ob_p_ref, oc_p_ref, x_p_ref, oa_s_ref, ob_s_ref, oc_s_ref, x_s_ref, w_ref, g_ref, b_ref,
                     wr_ref, br_ref, x1_ref, route_ref, *, prompt_tiles):
    prompt = pl.program_id(0) < prompt_tiles
    pick = lambda p_ref, s_ref: jnp.where(prompt, p_ref[...], s_ref[...])
    mix = (jnp.dot(pick(oa_p_ref, oa_s_ref).astype(BF16), w_ref[0:D_A, :], preferred_element_type=F32)
           + jnp.dot(pick(ob_p_ref, ob_s_ref).astype(BF16), w_ref[D_A:D_A + D_B, :], preferred_element_type=F32)
           + jnp.dot(pick(oc_p_ref, oc_s_ref).astype(BF16), w_ref[D_A + D_B:, :], preferred_element_type=F32))
    x1 = _layer_norm(DEEPNORM_ALPHA * pick(x_p_ref, x_s_ref) + mix, g_ref[...], b_ref[...])
    x1_ref[...] = x1
    route_ref[...] = _route(x1, wr_ref[...], br_ref[...])


def _out_proj(mix_p, mix_s, x_p, x_s, sample_first_row, w_out_bf16, g, b, wr, br, layer):
    n_p, n_s = mix_p[0].shape[0], mix_s[0].shape[0]
    n = n_p + n_s
    prompt_tiles = n_p // ROW_TILE
    first_s = sample_first_row // ROW_TILE
    row = lambda w: pl.BlockSpec((ROW_TILE, w), lambda i: (i, 0))
    prompt_row = lambda w: pl.BlockSpec((ROW_TILE, w), lambda i: (jnp.minimum(i, prompt_tiles - 1), 0))
    sample_row = lambda w, off=0: pl.BlockSpec((ROW_TILE, w), lambda i: (off + jnp.maximum(i - prompt_tiles, 0), 0))
    per_layer = lambda *shape: pl.BlockSpec((None,) + shape, lambda i: (layer,) + (0,) * len(shape))
    widths = (D_A, D_B, D_C)
    return pl.pallas_call(
        functools.partial(_out_proj_kernel, prompt_tiles=prompt_tiles),
        grid=(n // ROW_TILE,),
        in_specs=[prompt_row(w) for w in widths] + [prompt_row(D_MODEL)]
                 + [sample_row(w) for w in widths] + [sample_row(D_MODEL, first_s)]
                 + [per_layer(D_MODEL, D_MODEL), per_layer(1, D_MODEL), per_layer(1, D_MODEL),
                    per_layer(D_MODEL, LANES), per_layer(1, LANES)],
        out_specs=[row(D_MODEL), row(LANES)],
        out_shape=[jax.ShapeDtypeStruct((n, D_MODEL), F32), jax.ShapeDtypeStruct((n, LANES), F32)],
        compiler_params=_cparams("parallel"),
        name="out_proj_ln_route",
    )(*mix_p, x_p, *mix_s, x_s, w_out_bf16, g, b, wr, br)


def _moe_tiles(n_tokens):
    slots = 2 * n_tokens
    return -(-(slots + N_EXPERTS * (MOE_TILE - 1)) // MOE_TILE)


PLAN_BLOCK = 128


def _moe_plan(route, n_tokens):
    n_tiles = _moe_tiles(n_tokens)
    e = route[:, 0:2].astype(jnp.int32).reshape(-1, PLAN_BLOCK)
    onehot = (e[:, :, None] == jnp.arange(N_EXPERTS, dtype=jnp.int32)).astype(F32)
    tri = jnp.tril(jnp.ones((PLAN_BLOCK, PLAN_BLOCK), F32))
    within = jnp.einsum("ts,bse->bte", tri, onehot)
    block_total = within[:, -1, :]
    block_start = jnp.cumsum(block_total, axis=0) - block_total
    counts = (block_start[-1] + block_total[-1]).astype(jnp.int32)
    padded = (counts + MOE_TILE - 1) // MOE_TILE * MOE_TILE
    ends = jnp.cumsum(padded)
    starts = (ends - padded).astype(F32)
    pos = jnp.sum(onehot * (starts + block_start[:, None, :] + within - onehot), axis=-1)
    tile_start = jnp.arange(n_tiles, dtype=jnp.int32) * MOE_TILE
    tile_expert = jnp.sum((ends[None, :] <= tile_start[:, None]).astype(jnp.int32), axis=1)
    info = jnp.concatenate([ends, ends[-1:] // MOE_TILE]).astype(jnp.int32)
    return pos.astype(jnp.int32).reshape(-1), info, jnp.minimum(tile_expert, N_EXPERTS - 1)


def _dispatch_kernel(info_ref, pos_ref, x_ref, xs_hbm, zero_ref, sem, zero_sem, *, n_tiles):
    tokens = x_ref.shape[0]

    def zero_tile(row0):
        return pltpu.make_async_copy(zero_ref, xs_hbm.at[pl.ds(pl.multiple_of(row0, MOE_TILE), MOE_TILE), :], zero_sem)

    def for_each_pad_tile(act):
        for e in range(N_EXPERTS):
            start = info_ref[e - 1] if e else 0

            @pl.when(info_ref[e] > start)
            def _():
                act(zero_tile(info_ref[e] - MOE_TILE))

        def tail(t, _):
            act(zero_tile(t * MOE_TILE))
            return 0

        lax.fori_loop(info_ref[N_EXPERTS], n_tiles, tail, 0)

    @pl.when(pl.program_id(0) == 0)
    def _():
        zero_ref[...] = jnp.zeros_like(zero_ref)
        for_each_pad_tile(lambda cp: cp.start())
        for_each_pad_tile(lambda cp: cp.wait())

    def row_copy(t, dst_row):
        return pltpu.make_async_copy(x_ref.at[pl.ds(t, 1), :], xs_hbm.at[pl.ds(dst_row, 1), :], sem)

    def issue(t, _):
        row_copy(t, pos_ref[0, 0, 2 * t]).start()
        row_copy(t, pos_ref[0, 0, 2 * t + 1]).start()
        return 0

    def drain(t, _):
        row_copy(0, 0).wait()
        row_copy(0, 0).wait()
        return 0

    lax.fori_loop(0, tokens, issue, 0, unroll=4)
    lax.fori_loop(0, tokens, drain, 0, unroll=4)


def _dispatch(x1, pos, info):
    n = x1.shape[0]
    n_tiles = _moe_tiles(n)
    steps = n // ROW_TILE
    return pl.pallas_call(
        functools.partial(_dispatch_kernel, n_tiles=n_tiles),
        grid_spec=pltpu.PrefetchScalarGridSpec(
            num_scalar_prefetch=1,
            grid=(steps,),
            in_specs=[pl.BlockSpec((1, 1, 2 * ROW_TILE), lambda i, info: (i, 0, 0), memory_space=pltpu.SMEM),
                      pl.BlockSpec((ROW_TILE, D_MODEL), lambda i, info: (i, 0))],
            out_specs=pl.BlockSpec(memory_space=pl.ANY),
            scratch_shapes=[pltpu.VMEM((MOE_TILE, D_MODEL), F32), pltpu.SemaphoreType.DMA, pltpu.SemaphoreType.DMA]),
        out_shape=jax.ShapeDtypeStruct((n_tiles * MOE_TILE, D_MODEL), F32),
        compiler_params=_cparams("arbitrary"),
        name="moe_dispatch",
    )(info, pos.reshape(steps, 1, 2 * ROW_TILE), x1)


def _expert_kernel(te_ref, info_ref, x_ref, wg_ref, wu_ref, wd_ref, o_ref):
    used = pl.program_id(0) < info_ref[N_EXPERTS]

    @pl.when(used)
    def _():
        x = x_ref[...].astype(BF16)
        g = jnp.dot(x, wg_ref[...].astype(BF16), preferred_element_type=F32)
        u = jnp.dot(x, wu_ref[...].astype(BF16), preferred_element_type=F32)
        h = (g * _sigmoid(g)) * u
        o_ref[...] = jnp.dot(h.astype(BF16), wd_ref[...].astype(BF16), preferred_element_type=F32)

    @pl.when(jnp.logical_not(used))
    def _():
        o_ref[...] = jnp.zeros_like(o_ref)


def _experts(xs, tile_expert, info, w_gate, w_up, w_down, layer):
    rows = xs.shape[0]
    w_gate = w_gate.reshape(DEPTH * N_EXPERTS, D_MODEL, D_EXPERT)
    w_up = w_up.reshape(DEPTH * N_EXPERTS, D_MODEL, D_EXPERT)
    w_down = w_down.reshape(DEPTH * N_EXPERTS, D_EXPERT, D_MODEL)
    expert = lambda i, te, info: (layer * N_EXPERTS + te[i], 0, 0)
    return pl.pallas_call(
        _expert_kernel,
        grid_spec=pltpu.PrefetchScalarGridSpec(
            num_scalar_prefetch=2,
            grid=(rows // MOE_TILE,),
            in_specs=[pl.BlockSpec((MOE_TILE, D_MODEL), lambda i, te, info: (i, 0)),
                      pl.BlockSpec((None, D_MODEL, D_EXPERT), expert),
                      pl.BlockSpec((None, D_MODEL, D_EXPERT), expert),
                      pl.BlockSpec((None, D_EXPERT, D_MODEL), expert)],
            out_specs=pl.BlockSpec((MOE_TILE, D_MODEL), lambda i, te, info: (i, 0))),
        out_shape=jax.ShapeDtypeStruct((rows, D_MODEL), F32),
        compiler_params=_cparams("arbitrary"),
        name="experts",
    )(tile_expert, info, xs, w_gate, w_up, w_down)


COMBINE_TILE = 512


def _combine_ln_kernel(pos_ref, pos_next_ref, ys_hbm, x_ref, route_ref, g_ref, b_ref, o_ref, buf_ref, sem):
    tokens = x_ref.shape[0]
    i = pl.program_id(0)
    slot = i % 2

    def row_copy(src_row, dst_row, into):
        return pltpu.make_async_copy(ys_hbm.at[pl.ds(src_row, 1), :],
                                     buf_ref.at[into, pl.ds(dst_row, 1), :], sem.at[into])

    def issue(idx_ref, into):
        def body(t, _):
            row_copy(idx_ref[0, 0, 2 * t], t, into).start()
            row_copy(idx_ref[0, 0, 2 * t + 1], tokens + t, into).start()
            return 0
        lax.fori_loop(0, tokens, body, 0, unroll=4)

    @pl.when(i == 0)
    def _():
        issue(pos_ref, 0)

    @pl.when(i + 1 < pl.num_programs(0))
    def _():
        issue(pos_next_ref, 1 - slot)

    def drain(t, _):
        row_copy(0, 0, slot).wait()
        row_copy(0, 0, slot).wait()
        return 0

    lax.fori_loop(0, tokens, drain, 0, unroll=4)
    route = route_ref[...]
    y = route[:, 2:3] * buf_ref[slot, 0:tokens, :] + route[:, 3:4] * buf_ref[slot, tokens:, :]
    o_ref[...] = _layer_norm(DEEPNORM_ALPHA * x_ref[...] + y, g_ref[...], b_ref[...])


def _combine_ln(ys, pos, x1, route, g, b, layer):
    n = x1.shape[0]
    steps = n // COMBINE_TILE
    vec = pl.BlockSpec((None, 1, D_MODEL), lambda i: (layer, 0, 0))
    idx = lambda f: pl.BlockSpec((1, 1, 2 * COMBINE_TILE), lambda i: (f(i), 0, 0), memory_space=pltpu.SMEM)
    pos = pos.reshape(steps, 1, 2 * COMBINE_TILE)
    return pl.pallas_call(
        _combine_ln_kernel,
        grid=(steps,),
        in_specs=[idx(lambda i: i), idx(lambda i: jnp.minimum(i + 1, steps - 1)),
                  pl.BlockSpec(memory_space=pl.ANY),
                  pl.BlockSpec((COMBINE_TILE, D_MODEL), lambda i: (i, 0)),
                  pl.BlockSpec((COMBINE_TILE, LANES), lambda i: (i, 0)), vec, vec],
        out_specs=pl.BlockSpec((COMBINE_TILE, D_MODEL), lambda i: (i, 0)),
        out_shape=jax.ShapeDtypeStruct((n, D_MODEL), F32),
        scratch_shapes=[pltpu.VMEM((2, 2 * COMBINE_TILE, D_MODEL), F32), pltpu.SemaphoreType.DMA((2,))],
        compiler_params=_cparams("arbitrary"),
        name="moe_combine_ln",
    )(pos, pos, ys, x1, route, g, b)


def _moe(x1, route, p, layer):
    pos, info, tile_expert = _moe_plan(route, x1.shape[0])
    xs = _dispatch(x1, pos, info)
    ys = _experts(xs, tile_expert, info, p["moe_w_gate"], p["moe_w_up"], p["moe_w_down"], layer)
    return _combine_ln(ys, pos, x1, route, p["ln2_g3"], p["ln2_b3"], layer)


def kernel(x_prompt, x_sample, cache_win_k, cache_win_v, state_wkv, state_shift, w_in, w_out, sgu_ln_g, sgu_ln_b, sgu_ws, sgu_bias, rwkv_mu, rwkv_w0, rwkv_w2, rwkv_a0, rwkv_a2, rwkv_g2, rwkv_k_k, rwkv_k_a, rwkv_r_k, rwkv_lnx_g, rwkv_lnx_b, ln1_g, ln1_b, ln2_g, ln2_b, moe_router1, moe_router1_b, moe_router2, moe_router2_b, moe_w_gate, moe_w_up, moe_w_down):
    bp, tp, _ = x_prompt.shape
    bs, ts, _ = x_sample.shape
    n_p, n_s = bp * tp, bs * ts
    depth = w_in.shape[0]
    assert depth == DEPTH and tp <= DILATED_PATTERNS[-1][0] and tp % CHUNK == 0
    assert n_p % ROW_TILE == 0 and n_s % ROW_TILE == 0 and n_p % n_s == 0

    per_row = lambda a: a[:, None, :]
    p = dict(rwkv_mu=rwkv_mu, rwkv_w0=rwkv_w0, rwkv_a0=rwkv_a0, rwkv_k_k=rwkv_k_k, rwkv_k_a=rwkv_k_a,
             rwkv_r_k=rwkv_r_k, rwkv_lnx_g=rwkv_lnx_g, rwkv_lnx_b=rwkv_lnx_b,
             moe_w_gate=moe_w_gate, moe_w_up=moe_w_up, moe_w_down=moe_w_down,
             ln2_g3=per_row(ln2_g), ln2_b3=per_row(ln2_b))
    w_in_b = w_in.astype(BF16)
    w_kvt_b = jnp.swapaxes(w_in[:, :, D_A:3 * D_A], 1, 2).astype(BF16)
    w_out_b = w_out.astype(BF16)
    cache_kt, cache_vt = _window_transposed(cache_win_k), _window_transposed(cache_win_v)
    lora = _rwkv_lora_weights(rwkv_w2, rwkv_a2, rwkv_g2)
    wr, br = _router_weights(moe_router1, moe_router1_b, moe_router2, moe_router2_b)
    sgu_bias_full = _expand_sgu_bias(sgu_bias)
    sgu_coef = _sgu_sample_coef(sgu_ws, sgu_bias, ts)
    sgu_g, sgu_b = per_row(sgu_ln_g), per_row(sgu_ln_b)
    shift0 = state_shift[:, :, None, :]

    x_p, x_s, s_first = x_prompt.reshape(n_p, D_MODEL), x_sample.reshape(n_s, D_MODEL), 0
    outs = [[] for _ in range(9)]
    for layer in range(depth):
        q_p, kt_p, vt_p, uv_p, pc_p = _in_proj_prompt(x_p, bp, tp, w_in_b, w_kvt_b, layer)
        qkv_s, uv_s, pc_s = _in_proj_sample(x_s, s_first, n_s, w_in_b, layer)
        vec = _rwkv_vec_table(p, layer)

        oa_p = _attn_prompt(q_p, kt_p, vt_p, bp, tp)
        ob_p = _sgu_prompt(uv_p, n_p, sgu_ws, sgu_bias_full, sgu_g, sgu_b, layer)
        oc_p, wkv_p = _rwkv_prompt(pc_p, bp, tp, lora[layer], vec)

        pc_s = pc_s.reshape(bs, ts, D_C_PROJ)
        oa_s = _attn_sample(qkv_s, cache_kt, cache_vt, layer, bs, ts)
        ob_s, vn_s = _sgu_sample(uv_s, 0, n_s, sgu_coef, sgu_g, sgu_b, layer)
        oc_s, wkv_s = _rwkv_sample(jnp.pad(pc_s, ((0, 0), (0, 8 - ts), (0, 0))), shift0, state_wkv,
                                   lora[layer], vec, layer, ts)
        oc_s = oc_s[:, :ts].reshape(n_s, D_C)

        x1, route = _out_proj((oa_p, ob_p, oc_p), (oa_s, ob_s, oc_s), x_p, x_s, s_first, w_out_b,
                              per_row(ln1_g), per_row(ln1_b), wr, br, layer)
        x = _moe(x1, route, p, layer)
        x_p, x_s, s_first = x, x, n_p

        heads_p = lambda t: jnp.transpose(t.reshape(bp, N_HEADS_A, HEAD_DIM, tp), (0, 3, 1, 2))
        heads_s = lambda t: t.reshape(bs, ts, N_HEADS_A, HEAD_DIM)
        layer_outs = (heads_p(kt_p), heads_p(vt_p), wkv_p,
                      pc_p.reshape(bp, tp, D_C_PROJ)[:, -1],
                      heads_s(qkv_s[:, D_A:2 * D_A]), heads_s(qkv_s[:, 2 * D_A:]), wkv_s, pc_s[:, -1],
                      vn_s.reshape(bs, ts, D_B))
        for acc, o in zip(outs, layer_outs):
            acc.append(o)

    return (x[:n_p].reshape(bp, tp, D_MODEL), x[n_p:].reshape(bs, ts, D_MODEL)) + tuple(jnp.stack(o, 0) for o in outs)
```

```python
import functools
import math

import numpy as np
import jax
import jax.numpy as jnp
from jax import lax
from jax.experimental import pallas as pl
from jax.experimental.pallas import tpu as pltpu

F32 = jnp.float32
BF16 = jnp.bfloat16

D_MODEL = 1024
HEAD_DIM = 64
N_HEADS_A = 8
N_HEADS_B = 4
N_HEADS_C = 4
D_A = N_HEADS_A * HEAD_DIM
D_B = N_HEADS_B * HEAD_DIM
D_C = N_HEADS_C * HEAD_DIM
DILATED_PATTERNS = ((128, 1), (512, 4), (2048, 16))
CHUNK = 128
LORA_W, LORA_A, LORA_G = 32, 32, 64
D_LORA = LORA_W + LORA_A + LORA_G
D_C_PROJ = 3 * D_C + D_LORA
D_QKV = 3 * D_A
D_UV = 2 * D_B
D_IN = D_QKV + D_UV + D_C_PROJ
N_GROUPS = 4
EXPERTS_PER_GROUP = 8
N_EXPERTS = N_GROUPS * EXPERTS_PER_GROUP
D_EXPERT = 512
DEPTH = 2
DEEPNORM_ALPHA = (2 * DEPTH) ** 0.25
LN_EPS = 1e-5
GN_EPS = 64e-5
DECAY_SCALE = math.exp(-0.5)
L2_EPS = 1e-12
NEG = -1e30

LANES = 128
ROW_TILE = 512
Q_BLOCK = 128
RWKV_CHUNK = 64
MOE_TILE = 256
VMEM_LIMIT = 56 * 1024 * 1024


def _cparams(*sem):
    return pltpu.CompilerParams(dimension_semantics=sem, vmem_limit_bytes=VMEM_LIMIT)


def _dg(a, b, ca=1, cb=0):
    return lax.dot_general(a, b, (((ca,), (cb,)), ((), ())), preferred_element_type=F32)


def _split(x):
    hi = x.astype(BF16)
    lo = (x - hi.astype(F32)).astype(BF16)
    return hi, lo


def _dot3(a, b, ca=1, cb=0):
    ah, al = _split(a)
    bh, bl = _split(b)
    return _dg(ah, bh, ca, cb) + (_dg(ah, bl, ca, cb) + _dg(al, bh, ca, cb))


def _dot_exact_lhs(a_bf16, b, ca=1, cb=0):
    bh, bl = _split(b)
    return _dg(a_bf16, bh, ca, cb) + _dg(a_bf16, bl, ca, cb)


def _dot_exact_rhs(a, b_bf16, ca=1, cb=0):
    ah, al = _split(a)
    return _dg(ah, b_bf16, ca, cb) + _dg(al, b_bf16, ca, cb)


def _sigmoid(x):
    return 1.0 / (1.0 + jnp.exp(-x))


def _gelu(x):
    return 0.5 * x * (1.0 + lax.erf(x * (2.0 ** -0.5)))


def _layer_norm(z, g, b):
    mu = jnp.mean(z, axis=-1, keepdims=True)
    d = z - mu
    var = jnp.mean(d * d, axis=-1, keepdims=True)
    return d * lax.rsqrt(var + LN_EPS) * g + b


def _in_proj_sample_kernel(x_ref, w_ref, qkv_ref, uv_ref, pc_ref):
    x = x_ref[...].astype(BF16)
    qkv_ref[...] = jnp.dot(x, w_ref[:, 0:D_QKV], preferred_element_type=F32)
    uv_ref[...] = jnp.dot(x, w_ref[:, D_QKV:D_QKV + D_UV], preferred_element_type=F32)
    pc_ref[...] = jnp.dot(x, w_ref[:, D_QKV + D_UV:D_IN], preferred_element_type=F32)


def _in_proj_sample(x, first_row, n, w_in_bf16, layer):
    blk = first_row // n
    return pl.pallas_call(
        _in_proj_sample_kernel,
        grid=(1,),
        in_specs=[pl.BlockSpec((n, D_MODEL), lambda i: (blk, 0)),
                  pl.BlockSpec((None, D_MODEL, D_IN), lambda i: (layer, 0, 0))],
        out_specs=[pl.BlockSpec((n, w), lambda i: (0, 0)) for w in (D_QKV, D_UV, D_C_PROJ)],
        out_shape=[jax.ShapeDtypeStruct((n, w), F32) for w in (D_QKV, D_UV, D_C_PROJ)],
        compiler_params=_cparams("arbitrary"),
        name="in_proj_sample",
    )(x, w_in_bf16)


def _in_proj_prompt_kernel(x_ref, w_ref, wkvt_ref, q_ref, kt_ref, vt_ref, uv_ref, pc_ref):
    x = x_ref[...].astype(BF16)
    q_ref[...] = jnp.dot(x, w_ref[:, 0:D_A], preferred_element_type=F32)
    kt_ref[...] = _dg(wkvt_ref[0:D_A, :], x, 1, 1)
    vt_ref[...] = _dg(wkvt_ref[D_A:, :], x, 1, 1)
    uv_ref[...] = jnp.dot(x, w_ref[:, D_QKV:D_QKV + D_UV], preferred_element_type=F32)
    pc_ref[...] = jnp.dot(x, w_ref[:, D_QKV + D_UV:D_IN], preferred_element_type=F32)


def _in_proj_prompt(x, batch, seq, w_in_bf16, w_kvt_bf16, layer):
    n = batch * seq
    per_seq = seq // ROW_TILE
    row = lambda w: pl.BlockSpec((ROW_TILE, w), lambda i: (i, 0))
    tr = pl.BlockSpec((None, D_A, ROW_TILE), lambda i: (i // per_seq, 0, i % per_seq))
    return pl.pallas_call(
        _in_proj_prompt_kernel,
        grid=(n // ROW_TILE,),
        in_specs=[row(D_MODEL), pl.BlockSpec((None, D_MODEL, D_IN), lambda i: (layer, 0, 0)),
                  pl.BlockSpec((None, 2 * D_A, D_MODEL), lambda i: (layer, 0, 0))],
        out_specs=[row(D_A), tr, tr, row(D_UV), row(D_C_PROJ)],
        out_shape=[jax.ShapeDtypeStruct((n, D_A), F32), jax.ShapeDtypeStruct((batch, D_A, seq), F32),
                   jax.ShapeDtypeStruct((batch, D_A, seq), F32), jax.ShapeDtypeStruct((n, D_UV), F32),
                   jax.ShapeDtypeStruct((n, D_C_PROJ), F32)],
        compiler_params=_cparams("parallel"),
        name="in_proj_prompt",
    )(x, w_in_bf16, w_kvt_bf16)


def _pattern_count(delta):
    c = np.zeros(delta.shape, np.float64)
    for window, dilation in DILATED_PATTERNS:
        c += (delta >= 0) & (delta <= window) & (delta % dilation == 0)
    return c


def _log_count(delta):
    c = _pattern_count(delta)
    return np.where(c > 0, np.log(np.maximum(c, 1.0)), NEG).astype(np.float32)


def _prompt_bias_table(seq):
    i = np.arange(Q_BLOCK)[:, None]
    n = np.arange(seq)[None, :]
    tab = _log_count(seq - Q_BLOCK + i - n)
    return np.concatenate([tab, tab], axis=0)


def _attn_prompt_kernel(q_ref, kt_ref, vt_ref, bias_ref, o_ref, kb_ref, vb_ref, *, nb):
    seq = nb * Q_BLOCK
    kb_ref[...] = kt_ref[...].astype(BF16)
    vb_ref[...] = vt_ref[...].astype(BF16)
    first = lax.broadcasted_iota(jnp.int32, (Q_BLOCK, LANES), 1) < HEAD_DIM
    scale = HEAD_DIM ** -0.5
    for qb in range(nb):
        keys = (qb + 1) * Q_BLOCK
        q = q_ref[qb * Q_BLOCK:(qb + 1) * Q_BLOCK, :] * scale
        q2 = jnp.concatenate([jnp.where(first, q, 0.0), jnp.where(first, 0.0, q)], axis=0).astype(BF16)
        s = jnp.dot(q2, kb_ref[:, 0:keys], preferred_element_type=F32) + bias_ref[:, seq - keys:seq]
        p = jnp.exp(s - jnp.max(s, axis=-1, keepdims=True))
        denom = jnp.sum(p, axis=-1, keepdims=True)
        o = _dg(p.astype(BF16), vb_ref[:, 0:keys], 1, 1) / denom
        o_ref[qb * Q_BLOCK:(qb + 1) * Q_BLOCK, :] = jnp.where(first, o[:Q_BLOCK], o[Q_BLOCK:])


def _attn_prompt(q, kt, vt, batch, seq):
    nb = seq // Q_BLOCK
    pairs = D_A // LANES
    bias = jnp.asarray(_prompt_bias_table(seq))
    tr = pl.BlockSpec((None, LANES, seq), lambda b, p: (b, p, 0))
    return pl.pallas_call(
        functools.partial(_attn_prompt_kernel, nb=nb),
        grid=(batch, pairs),
        in_specs=[pl.BlockSpec((seq, LANES), lambda b, p: (b, p)), tr, tr,
                  pl.BlockSpec((2 * Q_BLOCK, seq), lambda b, p: (0, 0))],
        out_specs=pl.BlockSpec((seq, LANES), lambda b, p: (b, p)),
        out_shape=jax.ShapeDtypeStruct((batch * seq, D_A), F32),
        scratch_shapes=[pltpu.VMEM((LANES, seq), BF16), pltpu.VMEM((LANES, seq), BF16)],
        compiler_params=_cparams("parallel", "parallel"),
        name="attn_prompt",
    )(q, kt, vt, bias)


def _sgu_prompt_kernel(u_ref, v_ref, ws_ref, bias_ref, g_ref, b_ref, o_ref):
    u = _gelu(u_ref[...])
    vn = _layer_norm(_gelu(v_ref[...]), g_ref[...], b_ref[...]).astype(BF16)
    r = lax.broadcasted_iota(jnp.int32, (CHUNK, CHUNK), 0)
    c = lax.broadcasted_iota(jnp.int32, (CHUNK, CHUNK), 1)
    head = lax.broadcasted_iota(jnp.int32, (CHUNK, D_B), 1) // HEAD_DIM
    wm = [jnp.where(r >= c, ws_ref[h], 0.0).astype(BF16) for h in range(N_HEADS_B)]
    for j in range(u_ref.shape[0] // CHUNK):
        rows = slice(j * CHUNK, (j + 1) * CHUNK)
        z = bias_ref[...]
        for h in range(N_HEADS_B):
            z = z + jnp.where(head == h, jnp.dot(wm[h], vn[rows], preferred_element_type=F32), 0.0)
        o_ref[rows, :] = u[rows] * z


def _expand_sgu_bias(bias):
    return jnp.repeat(jnp.swapaxes(bias, 1, 2), HEAD_DIM, axis=2)


def _sgu_prompt(uv, n, ws, bias_full, g, b, layer):
    vec = pl.BlockSpec((None, 1, D_B), lambda i: (layer, 0, 0))
    rows = math.gcd(n, ROW_TILE)
    return pl.pallas_call(
        _sgu_prompt_kernel,
        grid=(n // rows,),
        in_specs=[pl.BlockSpec((rows, D_B), lambda i: (i, 0)), pl.BlockSpec((rows, D_B), lambda i: (i, 1)),
                  pl.BlockSpec((None, N_HEADS_B, CHUNK, CHUNK), lambda i: (layer, 0, 0, 0)),
                  pl.BlockSpec((None, CHUNK, D_B), lambda i: (layer, 0, 0)), vec, vec],
        out_specs=pl.BlockSpec((rows, D_B), lambda i: (i, 0)),
        out_shape=jax.ShapeDtypeStruct((n, D_B), F32),
        compiler_params=_cparams("parallel"),
        name="sgu_prompt",
    )(uv, uv, ws, bias_full, g, b)


def _rwkv_lora_weights(w2, a2, g2):
    depth = w2.shape[0]
    w = jnp.zeros((depth, D_LORA, 3 * D_C), F32)
    w = w.at[:, 0:LORA_W, 0:D_C].set(w2)
    w = w.at[:, LORA_W:LORA_W + LORA_A, D_C:2 * D_C].set(a2)
    return w.at[:, LORA_W + LORA_A:, 2 * D_C:].set(g2)


def _rwkv_features(xs, wl, vec):
    r = xs[:, 0:D_C]
    k = xs[:, D_C:2 * D_C]
    v = xs[:, 2 * D_C:3 * D_C]
    tail = xs[:, 3 * D_C:]
    lane = lax.broadcasted_iota(jnp.int32, tail.shape, 1)
    feats = jnp.where(lane < LORA_W, jnp.tanh(tail), jnp.where(lane < LORA_W + LORA_A, tail, _sigmoid(tail)))
    lora = _dot3(feats, wl)
    logw = -DECAY_SCALE * _sigmoid(vec["w0"] + lora[:, 0:D_C])
    a = _sigmoid(vec["a0"] + lora[:, D_C:2 * D_C])
    gate = lora[:, 2 * D_C:]
    kk = k * vec["k_k"]
    ri = lax.broadcasted_iota(jnp.int32, (D_C, D_C), 0) // HEAD_DIM
    ci = lax.broadcasted_iota(jnp.int32, (D_C, D_C), 1) // HEAD_DIM
    head_ones = jnp.where(ri == ci, 1.0, 0.0).astype(BF16)
    kk = kk / jnp.maximum(jnp.sqrt(_dot_exact_rhs(kk * kk, head_ones)), L2_EPS)
    k = k * (1.0 + (a - 1.0) * vec["k_a"])
    bonus = _dot_exact_rhs(r * k * vec["r_k"], head_ones) * v
    return r, k, v, logw, kk, kk * a, gate, bonus


def _rwkv_chunk_fn(E, C, n_valid, wl_ref, vec_ref):
    H = N_HEADS_C
    HC = H * C
    n = E * HC
    names = ("mu_r", "mu_k", "mu_v", "w0", "a0", "k_k", "k_a", "r_k", "lnx_g", "lnx_b")
    vec = {nm: vec_ref[i:i + 1, :] for i, nm in enumerate(names)}
    mu = jnp.concatenate([vec["mu_r"], vec["mu_k"], vec["mu_v"], vec_ref[len(names):len(names) + 1, 0:D_LORA]], axis=1)

    first_row = lax.broadcasted_iota(jnp.int32, (E * C, D_C_PROJ), 0) % C == 0
    is_token = lax.broadcasted_iota(jnp.int32, (E * C, D_C), 0) % C < n_valid
    srow = lax.broadcasted_iota(jnp.int32, (n, n), 0)
    scol = lax.broadcasted_iota(jnp.int32, (n, n), 1)
    same_block = (srow // C) == (scol // C)
    strict = same_block & ((srow % C) > (scol % C))
    incl = same_block & ((srow % C) >= (scol % C))
    eye = jnp.where(srow == scol, 1.0, 0.0)
    own = (lax.broadcasted_iota(jnp.int32, (1, H, 1, D_C), 1)
           == lax.broadcasted_iota(jnp.int32, (1, H, 1, D_C), 3) // HEAD_DIM)
    own_rows = jnp.broadcast_to(own, (E, H, C, D_C)).reshape(n, D_C)
    tr = lax.broadcasted_iota(jnp.int32, (E * C, E * C), 0)
    tc = lax.broadcasted_iota(jnp.int32, (E * C, E * C), 1)
    cum = jnp.where(((tr // C) == (tc // C)) & (tr >= tc), 1.0, 0.0).astype(BF16)

    def stack(x):
        x4 = jnp.broadcast_to(x.reshape(E, 1, C, D_C), (E, H, C, D_C))
        return jnp.where(own, x4, 0.0).reshape(n, D_C)

    def unstack(xs):
        x4 = xs.reshape(E, H, C, D_C)
        return ((x4[:, 0] + x4[:, 1]) + (x4[:, 2] + x4[:, 3])).reshape(E * C, D_C)

    def per_sequence(x, f):
        return jnp.broadcast_to(f(x.reshape(E, C, D_C)), (E, C, D_C)).reshape(E * C, D_C)

    def chunk(pc, prev_rows, s0):
        prev0 = jnp.broadcast_to(prev_rows, (E, C, D_C_PROJ)).reshape(E * C, D_C_PROJ)
        prev = jnp.where(first_row, prev0, pltpu.roll(pc, 1, axis=0))
        xs = pc + (prev - pc) * mu
        r, k, v, logw, kk, b, gate, bonus = _rwkv_features(xs, wl_ref[...], vec)
        if n_valid < C:
            logw = jnp.where(is_token, logw, 0.0)
            kk = jnp.where(is_token, kk, 0.0)
            b = jnp.where(is_token, b, 0.0)
            k = jnp.where(is_token, k, 0.0)

        lc = _dot_exact_lhs(cum, logw)
        lc_end = per_sequence(lc, lambda t: t[:, C - 1:C, :])
        w_inv = jnp.exp(-lc)
        a_t = stack(-kk * jnp.exp(lc - logw))
        r_t = stack(r * jnp.exp(lc))
        b_t = stack(b * w_inv)
        k_t = stack(k * w_inv)
        to_end = jnp.exp(lc_end - lc)
        b_e = stack(b * to_end)
        k_e = stack(k * to_end)
        v_s = stack(v)

        ar = jnp.concatenate([a_t, r_t], axis=0).astype(BF16)
        gram = _dg(ar, jnp.concatenate([b_t, k_t], axis=0).astype(BF16), 1, 1)
        ab = jnp.where(strict, gram[:n, :n], 0.0)
        ak = jnp.where(strict, gram[:n, n:], 0.0).astype(BF16)
        rb = jnp.where(incl, gram[n:, :n], 0.0).astype(BF16)
        rk = jnp.where(incl, gram[n:, n:], 0.0).astype(BF16)

        inv = eye + ab
        power = ab.astype(BF16)
        for step in range(int(math.log2(C)) - 1):
            power = _dg(power, power)
            inv = inv + _dg(power.astype(BF16), inv.astype(BF16))
            power = power.astype(BF16)

        seq_rows = lambda x, e, off=0: x[off + e * HC:off + (e + 1) * HC]
        from_s0 = [_dg(jnp.concatenate([seq_rows(ar, e), seq_rows(ar, e, n)], axis=0), s0[e].astype(BF16), 1, 1)
                   for e in range(E)]
        a_s0 = jnp.concatenate([f[:HC] for f in from_s0], axis=0)
        r_s0 = jnp.concatenate([f[HC:] for f in from_s0], axis=0)
        v_b = v_s.astype(BF16)
        u_b = _dg(inv.astype(BF16), (a_s0 + _dg(ak, v_b)).astype(BF16)).astype(BF16)
        y = unstack(r_s0 + _dg(rb, u_b) + _dg(rk, v_b))
        b_e = b_e.astype(BF16)
        k_e = k_e.astype(BF16)
        s_end = [s0[e] * jnp.exp(lc_end[e * C:e * C + 1]) + _dg(seq_rows(u_b, e), seq_rows(b_e, e), 0, 0)
                 + _dg(seq_rows(v_b, e), seq_rows(k_e, e), 0, 0) for e in range(E)]

        ys = stack(y)
        mean = jnp.sum(ys, axis=-1, keepdims=True) * (1.0 / HEAD_DIM)
        d = jnp.where(own_rows, ys - mean, 0.0)
        var = jnp.sum(d * d, axis=-1, keepdims=True) * (1.0 / HEAD_DIM)
        yn = unstack(d * lax.rsqrt(var + GN_EPS)) * vec["lnx_g"] + vec["lnx_b"]
        return (yn + bonus) * gate, s_end

    return chunk


def _store_head_states(s_out_ref, s):
    for h in range(N_HEADS_C):
        s_out_ref[h] = s[h * HEAD_DIM:(h + 1) * HEAD_DIM, h * HEAD_DIM:(h + 1) * HEAD_DIM]


def _rwkv_prompt_kernel(pc_ref, wl_ref, vec_ref, o_ref, s_out_ref, s_ref, prev_ref):
    C = RWKV_CHUNK
    group, span, _ = pc_ref.shape
    chunk = _rwkv_chunk_fn(1, C, C, wl_ref, vec_ref)

    @pl.when(pl.program_id(1) == 0)
    def _():
        s_ref[...] = jnp.zeros_like(s_ref)
        prev_ref[...] = jnp.zeros_like(prev_ref)

    def step(c, _):
        t0 = pl.multiple_of(c * C, C)
        for g in range(group):
            pc = pc_ref[g, pl.ds(t0, C), :]
            out, (s_end,) = chunk(pc, prev_ref[g][None], [s_ref[g]])
            prev_ref[g] = pc[C - 1:C, :]
            s_ref[g] = s_end
            o_ref[g, pl.ds(t0, C), :] = out
        return 0

    lax.fori_loop(0, span // C, step, 0)

    @pl.when(pl.program_id(1) == pl.num_programs(1) - 1)
    def _():
        for g in range(group):
            _store_head_states(s_out_ref.at[g], s_ref[g])


def _rwkv_sample_kernel(pc_ref, shift_ref, s_in_ref, wl_ref, vec_ref, o_ref, s_out_ref, *, n_valid):
    group, C, _ = pc_ref.shape
    zero = jnp.zeros((HEAD_DIM, HEAD_DIM), F32)
    s0 = [jnp.concatenate(
        [jnp.concatenate([s_in_ref[e, h] if g == h else zero for g in range(N_HEADS_C)], axis=1)
         for h in range(N_HEADS_C)], axis=0) for e in range(group)]
    chunk = _rwkv_chunk_fn(group, C, n_valid, wl_ref, vec_ref)
    out, s_end = chunk(pc_ref[...].reshape(group * C, D_C_PROJ), shift_ref[...], s0)
    o_ref[...] = out.reshape(group, C, D_C)
    for e in range(group):
        _store_head_states(s_out_ref.at[e], s_end[e])


RWKV_SAMPLE_GROUP = 8


def _rwkv_sample(pc_pad, shift0, wkv0, wl, vec, layer, n_valid):
    batch, rows, _ = pc_pad.shape
    group = math.gcd(batch, RWKV_SAMPLE_GROUP)
    state = pl.BlockSpec((None, group, N_HEADS_C, HEAD_DIM, HEAD_DIM), lambda b: (layer, b, 0, 0, 0))
    return pl.pallas_call(
        functools.partial(_rwkv_sample_kernel, n_valid=n_valid),
        grid=(batch // group,),
        in_specs=[pl.BlockSpec((group, rows, D_C_PROJ), lambda b: (b, 0, 0)),
                  pl.BlockSpec((None, group, 1, D_C_PROJ), lambda b: (layer, b, 0, 0)),
                  state,
                  pl.BlockSpec((D_LORA, 3 * D_C), lambda b: (0, 0)),
                  pl.BlockSpec((16, D_C), lambda b: (0, 0))],
        out_specs=[pl.BlockSpec((group, rows, D_C), lambda b: (b, 0, 0)),
                   pl.BlockSpec((group, N_HEADS_C, HEAD_DIM, HEAD_DIM), lambda b: (b, 0, 0, 0))],
        out_shape=[jax.ShapeDtypeStruct((batch, rows, D_C), F32),
                   jax.ShapeDtypeStruct((batch, N_HEADS_C, HEAD_DIM, HEAD_DIM), F32)],
        compiler_params=_cparams("parallel"),
        name="rwkv_sample",
    )(pc_pad, shift0, wkv0, wl, vec)


def _rwkv_vec_table(p, layer):
    mu = p["rwkv_mu"][layer]
    rows = [mu[0:D_C], mu[D_C:2 * D_C], mu[2 * D_C:3 * D_C], p["rwkv_w0"][layer], p["rwkv_a0"][layer],
            p["rwkv_k_k"][layer], p["rwkv_k_a"][layer], p["rwkv_r_k"][layer].reshape(D_C),
            p["rwkv_lnx_g"][layer], p["rwkv_lnx_b"][layer], jnp.pad(mu[3 * D_C:], (0, D_C - D_LORA))]
    rows += [jnp.zeros((D_C,), F32)] * (16 - len(rows))
    return jnp.stack(rows, 0)


RWKV_GROUP = 8
RWKV_SPAN = 256


def _rwkv_prompt(pc, batch, seq, wl, vec):
    group = math.gcd(batch, RWKV_GROUP)
    span = math.gcd(seq, RWKV_SPAN)
    o, s = pl.pallas_call(
        _rwkv_prompt_kernel,
        grid=(batch // group, seq // span),
        in_specs=[pl.BlockSpec((group, span, D_C_PROJ), lambda b, t: (b, t, 0)),
                  pl.BlockSpec((D_LORA, 3 * D_C), lambda b, t: (0, 0)),
                  pl.BlockSpec((16, D_C), lambda b, t: (0, 0))],
        out_specs=[pl.BlockSpec((group, span, D_C), lambda b, t: (b, t, 0)),
                   pl.BlockSpec((group, N_HEADS_C, HEAD_DIM, HEAD_DIM), lambda b, t: (b, 0, 0, 0))],
        out_shape=[jax.ShapeDtypeStruct((batch, seq, D_C), F32),
                   jax.ShapeDtypeStruct((batch, N_HEADS_C, HEAD_DIM, HEAD_DIM), F32)],
        scratch_shapes=[pltpu.VMEM((group, D_C, D_C), F32), pltpu.VMEM((group, 1, D_C_PROJ), F32)],
        compiler_params=_cparams("parallel", "arbitrary"),
        name="rwkv_prompt",
    )(pc[:batch * seq].reshape(batch, seq, D_C_PROJ), wl, vec)
    return o.reshape(batch * seq, D_C), s


NEW_PAD = 8


def _sample_bias_tables(w_buf, dec_seq):
    q_pos = w_buf + np.repeat(np.arange(dec_seq), N_HEADS_A)[:, None]
    window = _log_count(q_pos - np.arange(w_buf)[None, :])
    new = _log_count(q_pos - (w_buf + np.arange(NEW_PAD))[None, :])
    new[:, dec_seq:] = NEG
    return window, new


def _attn_sample_kernel(q_ref, knew_ref, vnew_ref, kt_ref, vt_ref, bias_w_ref, bias_n_ref, o_ref):
    rows = q_ref.shape[0]
    dec_seq = rows // N_HEADS_A
    q = q_ref[...].astype(BF16)
    s_w = jnp.dot(q, kt_ref[...].astype(BF16), preferred_element_type=F32) + bias_w_ref[...]
    s_n = _dg(q, knew_ref[...].astype(BF16), 1, 1) + bias_n_ref[...]
    m = jnp.maximum(jnp.max(s_w, axis=-1, keepdims=True), jnp.max(s_n, axis=-1, keepdims=True))
    p_w = jnp.exp(s_w - m)
    p_n = jnp.exp(s_n - m)
    denom = jnp.sum(p_w, axis=-1, keepdims=True) + jnp.sum(p_n, axis=-1, keepdims=True)
    o = (_dg(p_w.astype(BF16), vt_ref[...].astype(BF16), 1, 1)
         + jnp.dot(p_n.astype(BF16), vnew_ref[...].astype(BF16), preferred_element_type=F32)) / denom
    head = lax.broadcasted_iota(jnp.int32, (N_HEADS_A, D_A), 0)
    col_head = lax.broadcasted_iota(jnp.int32, (N_HEADS_A, D_A), 1) // HEAD_DIM
    for i in range(dec_seq):
        tile = o[i * N_HEADS_A:(i + 1) * N_HEADS_A, :]
        o_ref[i:i + 1, :] = jnp.sum(jnp.where(head == col_head, tile, 0.0), axis=0, keepdims=True)


def _attn_sample(qkv_s, cache_kt, cache_vt, layer, batch, dec_seq):
    w_buf = cache_kt.shape[3]
    assert dec_seq <= NEW_PAD
    rows = dec_seq * N_HEADS_A
    q = qkv_s[:, 0:D_A].reshape(batch, dec_seq, N_HEADS_A, 1, HEAD_DIM) * HEAD_DIM ** -0.5
    q_bd = (q * jnp.eye(N_HEADS_A, dtype=F32)[None, None, :, :, None]).reshape(batch, rows, D_A)
    pad_new = lambda t: jnp.pad(t.reshape(batch, dec_seq, D_A), ((0, 0), (0, NEW_PAD - dec_seq), (0, 0)))
    bias_w, bias_n = (jnp.asarray(t) for t in _sample_bias_tables(w_buf, dec_seq))
    cache = pl.BlockSpec((None, None, D_A, w_buf), lambda b: (layer, b, 0, 0))
    new = pl.BlockSpec((None, NEW_PAD, D_A), lambda b: (b, 0, 0))
    o = pl.pallas_call(
        _attn_sample_kernel,
        grid=(batch,),
        in_specs=[pl.BlockSpec((None, rows, D_A), lambda b: (b, 0, 0)), new, new, cache, cache,
                  pl.BlockSpec(bias_w.shape, lambda b: (0, 0)), pl.BlockSpec(bias_n.shape, lambda b: (0, 0))],
        out_specs=pl.BlockSpec((None, dec_seq, D_A), lambda b: (b, 0, 0)),
        out_shape=jax.ShapeDtypeStruct((batch, dec_seq, D_A), F32),
        compiler_params=_cparams("parallel"),
        name="attn_sample",
    )(q_bd, pad_new(qkv_s[:, D_A:2 * D_A]), pad_new(qkv_s[:, 2 * D_A:]), cache_kt, cache_vt, bias_w, bias_n)
    return o.reshape(batch * dec_seq, D_A)


def _window_transposed(cache):
    depth, batch, w_buf = cache.shape[:3]
    return jnp.transpose(cache, (0, 1, 3, 4, 2)).reshape(depth, batch, D_A, w_buf)


def _sgu_sample_coef(ws, bias, dec_seq):
    assert 8 % dec_seq == 0
    t = np.arange(8) % dec_seq
    out = []
    for k in range(dec_seq):
        w = ws[:, :, t, np.maximum(t - k, 0)] * jnp.asarray(t >= k, F32)
        out.append(w)
    out.append(bias[:, :, t])
    coef = jnp.stack(out, axis=1)
    return jnp.repeat(jnp.swapaxes(coef, 2, 3), HEAD_DIM, axis=3)


def _sgu_sample_kernel(u_ref, v_ref, coef_ref, g_ref, b_ref, o_ref, vn_ref):
    n, width = u_ref.shape
    taps = coef_ref.shape[0] - 1
    u = _gelu(u_ref[...])
    vn = _layer_norm(_gelu(v_ref[...]), g_ref[...], b_ref[...])
    vn_ref[...] = vn
    z = jnp.broadcast_to(coef_ref[taps][None], (n // 8, 8, width))
    for k in range(taps):
        shifted = vn if k == 0 else pltpu.roll(vn, k, axis=0)
        z = z + coef_ref[k][None] * shifted.reshape(n // 8, 8, width)
    o_ref[...] = u * z.reshape(n, width)


def _sgu_sample(uv, first_row, n, coef, g, b, layer):
    blk = first_row // n
    vec = pl.BlockSpec((None, 1, D_B), lambda i: (layer, 0, 0))
    return pl.pallas_call(
        _sgu_sample_kernel,
        grid=(1,),
        in_specs=[pl.BlockSpec((n, D_B), lambda i: (blk, 0)), pl.BlockSpec((n, D_B), lambda i: (blk, 1)),
                  pl.BlockSpec((None,) + coef.shape[1:], lambda i: (layer, 0, 0, 0)), vec, vec],
        out_specs=[pl.BlockSpec((n, D_B), lambda i: (0, 0)), pl.BlockSpec((n, D_B), lambda i: (0, 0))],
        out_shape=[jax.ShapeDtypeStruct((n, D_B), F32), jax.ShapeDtypeStruct((n, D_B), F32)],
        compiler_params=_cparams("arbitrary"),
        name="sgu_sample",
    )(uv, uv, coef, g, b)


def _router_weights(r1, r1b, r2, r2b):
    depth = r1.shape[0]
    w = jnp.concatenate([r1, r2, jnp.zeros((depth, D_MODEL, LANES - N_GROUPS - N_EXPERTS), F32)], axis=-1)
    b = jnp.concatenate([r1b, r2b.reshape(depth, N_EXPERTS), jnp.zeros((depth, LANES - N_GROUPS - N_EXPERTS), F32)], axis=-1)
    return w, b[:, None, :]


def _route(x, wr, br):
    logits = _dot3(x, wr) + br
    lane_i = lax.broadcasted_iota(jnp.int32, logits.shape, 1)
    lane = lane_i.astype(F32)
    far = float(LANES)
    is_g = lane_i < N_GROUPS
    lg = jnp.where(is_g, logits, NEG)
    gmax = jnp.max(lg, axis=-1, keepdims=True)
    grp = jnp.min(jnp.where(lg == gmax, lane, far), axis=-1, keepdims=True)
    gate = 1.0 / jnp.sum(jnp.where(is_g, jnp.exp(lg - gmax), 0.0), axis=-1, keepdims=True)
    lo = N_GROUPS + EXPERTS_PER_GROUP * grp
    le = jnp.where((lane >= lo) & (lane < lo + EXPERTS_PER_GROUP), logits, NEG)
    t1 = jnp.max(le, axis=-1, keepdims=True)
    i1 = jnp.min(jnp.where(le == t1, lane, far), axis=-1, keepdims=True)
    le2 = jnp.where(lane == i1, NEG, le)
    t2 = jnp.max(le2, axis=-1, keepdims=True)
    i2 = jnp.min(jnp.where(le2 == t2, lane, far), axis=-1, keepdims=True)
    e = jnp.exp(t2 - t1)
    w1 = gate / (1.0 + e)
    w2 = gate * e / (1.0 + e)
    return jnp.where(lane_i == 0, i1 - N_GROUPS, jnp.where(lane_i == 1, i2 - N_GROUPS,
           jnp.where(lane_i == 2, w1, jnp.where(lane_i == 3, w2, 0.0))))


def _out_proj_kernel(oa_p_ref, ob_p_ref, oc_p_ref, x_p_ref, oa_s_ref, ob_s_ref, oc_s_ref, x_s_ref, w_ref, g_ref, b_ref,
                     wr_ref, br_ref, x1_ref, route_ref, *, prompt_tiles):
    prompt = pl.program_id(0) < prompt_tiles
    pick = lambda p_ref, s_ref: jnp.where(prompt, p_ref[...], s_ref[...])
    mix = (jnp.dot(pick(oa_p_ref, oa_s_ref).astype(BF16), w_ref[0:D_A, :], preferred_element_type=F32)
           + jnp.dot(pick(ob_p_ref, ob_s_ref).astype(BF16), w_ref[D_A:D_A + D_B, :], preferred_element_type=F32)
           + jnp.dot(pick(oc_p_ref, oc_s_ref).astype(BF16), w_ref[D_A + D_B:, :], preferred_element_type=F32))
    x1 = _layer_norm(DEEPNORM_ALPHA * pick(x_p_ref, x_s_ref) + mix, g_ref[...], b_ref[...])
    x1_ref[...] = x1
    route_ref[...] = _route(x1, wr_ref[...], br_ref[...])


def _out_proj(mix_p, mix_s, x_p, x_s, sample_first_row, w_out_bf16, g, b, wr, br, layer):
    n_p, n_s = mix_p[0].shape[0], mix_s[0].shape[0]
    n = n_p + n_s
    prompt_tiles = n_p // ROW_TILE
    first_s = sample_first_row // ROW_TILE
    row = lambda w: pl.BlockSpec((ROW_TILE, w), lambda i: (i, 0))
    prompt_row = lambda w: pl.BlockSpec((ROW_TILE, w), lambda i: (jnp.minimum(i, prompt_tiles - 1), 0))
    sample_row = lambda w, off=0: pl.BlockSpec((ROW_TILE, w), lambda i: (off + jnp.maximum(i - prompt_tiles, 0), 0))
    per_layer = lambda *shape: pl.BlockSpec((None,) + shape, lambda i: (layer,) + (0,) * len(shape))
    widths = (D_A, D_B, D_C)
    return pl.pallas_call(
        functools.partial(_out_proj_kernel, prompt_tiles=prompt_tiles),
        grid=(n // ROW_TILE,),
        in_specs=[prompt_row(w) for w in widths] + [prompt_row(D_MODEL)]
                 + [sample_row(w) for w in widths] + [sample_row(D_MODEL, first_s)]
                 + [per_layer(D_MODEL, D_MODEL), per_layer(1, D_MODEL), per_layer(1, D_MODEL),
                    per_layer(D_MODEL, LANES), per_layer(1, LANES)],
        out_specs=[row(D_MODEL), row(LANES)],
        out_shape=[jax.ShapeDtypeStruct((n, D_MODEL), F32), jax.ShapeDtypeStruct((n, LANES), F32)],
        compiler_params=_cparams("parallel"),
        name="out_proj_ln_route",
    )(*mix_p, x_p, *mix_s, x_s, w_out_bf16, g, b, wr, br)


def _moe_tiles(n_tokens):
    slots = 2 * n_tokens
    return -(-(slots + N_EXPERTS * (MOE_TILE - 1)) // MOE_TILE)


PLAN_BLOCK = 128


def _moe_plan(route, n_tokens):
    n_tiles = _moe_tiles(n_tokens)
    e = route[:, 0:2].astype(jnp.int32).reshape(-1, PLAN_BLOCK)
    onehot = (e[:, :, None] == jnp.arange(N_EXPERTS, dtype=jnp.int32)).astype(F32)
    tri = jnp.tril(jnp.ones((PLAN_BLOCK, PLAN_BLOCK), F32))
    within = jnp.einsum("ts,bse->bte", tri, onehot)
    block_total = within[:, -1, :]
    block_start = jnp.cumsum(block_total, axis=0) - block_total
    counts = (block_start[-1] + block_total[-1]).astype(jnp.int32)
    padded = (counts + MOE_TILE - 1) // MOE_TILE * MOE_TILE
    ends = jnp.cumsum(padded)
    starts = (ends - padded).astype(F32)
    pos = jnp.sum(onehot * (starts + block_start[:, None, :] + within - onehot), axis=-1)
    tile_start = jnp.arange(n_tiles, dtype=jnp.int32) * MOE_TILE
    tile_expert = jnp.sum((ends[None, :] <= tile_start[:, None]).astype(jnp.int32), axis=1)
    info = jnp.concatenate([ends, ends[-1:] // MOE_TILE]).astype(jnp.int32)
    return pos.astype(jnp.int32).reshape(-1), info, jnp.minimum(tile_expert, N_EXPERTS - 1)


def _dispatch_kernel(info_ref, pos_ref, x_ref, xs_hbm, zero_ref, sem, zero_sem, *, n_tiles):
    tokens = x_ref.shape[0]

    def zero_tile(row0):
        return pltpu.make_async_copy(zero_ref, xs_hbm.at[pl.ds(pl.multiple_of(row0, MOE_TILE), MOE_TILE), :], zero_sem)

    def for_each_pad_tile(act):
        for e in range(N_EXPERTS):
            start = info_ref[e - 1] if e else 0

            @pl.when(info_ref[e] > start)
            def _():
                act(zero_tile(info_ref[e] - MOE_TILE))

        def tail(t, _):
            act(zero_tile(t * MOE_TILE))
            return 0

        lax.fori_loop(info_ref[N_EXPERTS], n_tiles, tail, 0)

    @pl.when(pl.program_id(0) == 0)
    def _():
        zero_ref[...] = jnp.zeros_like(zero_ref)
        for_each_pad_tile(lambda cp: cp.start())
        for_each_pad_tile(lambda cp: cp.wait())

    def row_copy(t, dst_row):
        return pltpu.make_async_copy(x_ref.at[pl.ds(t, 1), :], xs_hbm.at[pl.ds(dst_row, 1), :], sem)

    def issue(t, _):
        row_copy(t, pos_ref[0, 0, 2 * t]).start(priority=0)
        row_copy(t, pos_ref[0, 0, 2 * t + 1]).start(priority=1)
        return 0

    def drain(t, _):
        row_copy(0, 0).wait()
        row_copy(0, 0).wait()
        return 0

    lax.fori_loop(0, tokens, issue, 0, unroll=4)
    lax.fori_loop(0, tokens, drain, 0, unroll=4)


def _dispatch(x1, pos, info):
    n = x1.shape[0]
    n_tiles = _moe_tiles(n)
    steps = n // ROW_TILE
    return pl.pallas_call(
        functools.partial(_dispatch_kernel, n_tiles=n_tiles),
        grid_spec=pltpu.PrefetchScalarGridSpec(
            num_scalar_prefetch=1,
            grid=(steps,),
            in_specs=[pl.BlockSpec((1, 1, 2 * ROW_TILE), lambda i, info: (i, 0, 0), memory_space=pltpu.SMEM),
                      pl.BlockSpec((ROW_TILE, D_MODEL), lambda i, info: (i, 0))],
            out_specs=pl.BlockSpec(memory_space=pl.ANY),
            scratch_shapes=[pltpu.VMEM((MOE_TILE, D_MODEL), F32), pltpu.SemaphoreType.DMA, pltpu.SemaphoreType.DMA]),
        out_shape=jax.ShapeDtypeStruct((n_tiles * MOE_TILE, D_MODEL), F32),
        compiler_params=_cparams("arbitrary"),
        name="moe_dispatch",
    )(info, pos.reshape(steps, 1, 2 * ROW_TILE), x1)


def _expert_kernel(te_ref, info_ref, x_ref, wg_ref, wu_ref, wd_ref, o_ref):
    used = pl.program_id(0) < info_ref[N_EXPERTS]

    @pl.when(used)
    def _():
        x = x_ref[...].astype(BF16)
        g = jnp.dot(x, wg_ref[...].astype(BF16), preferred_element_type=F32)
        u = jnp.dot(x, wu_ref[...].astype(BF16), preferred_element_type=F32)
        h = (g * _sigmoid(g)) * u
        o_ref[...] = jnp.dot(h.astype(BF16), wd_ref[...].astype(BF16), preferred_element_type=F32)

    @pl.when(jnp.logical_not(used))
    def _():
        o_ref[...] = jnp.zeros_like(o_ref)


def _experts(xs, tile_expert, info, w_gate, w_up, w_down, layer):
    rows = xs.shape[0]
    w_gate = w_gate.reshape(DEPTH * N_EXPERTS, D_MODEL, D_EXPERT)
    w_up = w_up.reshape(DEPTH * N_EXPERTS, D_MODEL, D_EXPERT)
    w_down = w_down.reshape(DEPTH * N_EXPERTS, D_EXPERT, D_MODEL)
    expert = lambda i, te, info: (layer * N_EXPERTS + te[i], 0, 0)
    return pl.pallas_call(
        _expert_kernel,
        grid_spec=pltpu.PrefetchScalarGridSpec(
            num_scalar_prefetch=2,
            grid=(rows // MOE_TILE,),
            in_specs=[pl.BlockSpec((MOE_TILE, D_MODEL), lambda i, te, info: (i, 0)),
                      pl.BlockSpec((None, D_MODEL, D_EXPERT), expert),
                      pl.BlockSpec((None, D_MODEL, D_EXPERT), expert),
                      pl.BlockSpec((None, D_EXPERT, D_MODEL), expert)],
            out_specs=pl.BlockSpec((MOE_TILE, D_MODEL), lambda i, te, info: (i, 0))),
        out_shape=jax.ShapeDtypeStruct((rows, D_MODEL), F32),
        compiler_params=_cparams("arbitrary"),
        name="experts",
    )(tile_expert, info, xs, w_gate, w_up, w_down)


COMBINE_TILE = 512


def _combine_ln_kernel(pos_ref, pos_next_ref, ys_hbm, x_ref, route_ref, g_ref, b_ref, o_ref, buf_ref, sem):
    tokens = x_ref.shape[0]
    i = pl.program_id(0)
    slot = i % 2

    def row_copy(src_row, dst_row, into):
        return pltpu.make_async_copy(ys_hbm.at[pl.ds(src_row, 1), :],
                                     buf_ref.at[into, pl.ds(dst_row, 1), :], sem.at[into])

    def issue(idx_ref, into):
        def body(t, _):
            row_copy(idx_ref[0, 0, 2 * t], t, into).start(priority=0)
            row_copy(idx_ref[0, 0, 2 * t + 1], tokens + t, into).start(priority=1)
            return 0
        lax.fori_loop(0, tokens, body, 0, unroll=4)

    @pl.when(i == 0)
    def _():
        issue(pos_ref, 0)

    @pl.when(i + 1 < pl.num_programs(0))
    def _():
        issue(pos_next_ref, 1 - slot)

    def drain(t, _):
        row_copy(0, 0, slot).wait()
        row_copy(0, 0, slot).wait()
        return 0

    lax.fori_loop(0, tokens, drain, 0, unroll=4)
    route = route_ref[...]
    y = route[:, 2:3] * buf_ref[slot, 0:tokens, :] + route[:, 3:4] * buf_ref[slot, tokens:, :]
    o_ref[...] = _layer_norm(DEEPNORM_ALPHA * x_ref[...] + y, g_ref[...], b_ref[...])


def _combine_ln(ys, pos, x1, route, g, b, layer):
    n = x1.shape[0]
    steps = n // COMBINE_TILE
    vec = pl.BlockSpec((None, 1, D_MODEL), lambda i: (layer, 0, 0))
    idx = lambda f: pl.BlockSpec((1, 1, 2 * COMBINE_TILE), lambda i: (f(i), 0, 0), memory_space=pltpu.SMEM)
    pos = pos.reshape(steps, 1, 2 * COMBINE_TILE)
    return pl.pallas_call(
        _combine_ln_kernel,
        grid=(steps,),
        in_specs=[idx(lambda i: i), idx(lambda i: jnp.minimum(i + 1, steps - 1)),
                  pl.BlockSpec(memory_space=pl.ANY),
                  pl.BlockSpec((COMBINE_TILE, D_MODEL), lambda i: (i, 0)),
                  pl.BlockSpec((COMBINE_TILE, LANES), lambda i: (i, 0)), vec, vec],
        out_specs=pl.BlockSpec((COMBINE_TILE, D_MODEL), lambda i: (i, 0)),
        out_shape=jax.ShapeDtypeStruct((n, D_MODEL), F32),
        scratch_shapes=[pltpu.VMEM((2, 2 * COMBINE_TILE, D_MODEL), F32), pltpu.SemaphoreType.DMA((2,))],
        compiler_params=_cparams("arbitrary"),
        name="moe_combine_ln",
    )(pos, pos, ys, x1, route, g, b)


def _moe(x1, route, p, layer):
    pos, info, tile_expert = _moe_plan(route, x1.shape[0])
    xs = _dispatch(x1, pos, info)
    ys = _experts(xs, tile_expert, info, p["moe_w_gate"], p["moe_w_up"], p["moe_w_down"], layer)
    return _combine_ln(ys, pos, x1, route, p["ln2_g3"], p["ln2_b3"], layer)


def kernel(x_prompt, x_sample, cache_win_k, cache_win_v, state_wkv, state_shift, w_in, w_out, sgu_ln_g, sgu_ln_b, sgu_ws, sgu_bias, rwkv_mu, rwkv_w0, rwkv_w2, rwkv_a0, rwkv_a2, rwkv_g2, rwkv_k_k, rwkv_k_a, rwkv_r_k, rwkv_lnx_g, rwkv_lnx_b, ln1_g, ln1_b, ln2_g, ln2_b, moe_router1, moe_router1_b, moe_router2, moe_router2_b, moe_w_gate, moe_w_up, moe_w_down):
    bp, tp, _ = x_prompt.shape
    bs, ts, _ = x_sample.shape
    n_p, n_s = bp * tp, bs * ts
    depth = w_in.shape[0]
    assert depth == DEPTH and tp <= DILATED_PATTERNS[-1][0] and tp % CHUNK == 0
    assert n_p % ROW_TILE == 0 and n_s % ROW_TILE == 0 and n_p % n_s == 0

    per_row = lambda a: a[:, None, :]
    p = dict(rwkv_mu=rwkv_mu, rwkv_w0=rwkv_w0, rwkv_a0=rwkv_a0, rwkv_k_k=rwkv_k_k, rwkv_k_a=rwkv_k_a,
             rwkv_r_k=rwkv_r_k, rwkv_lnx_g=rwkv_lnx_g, rwkv_lnx_b=rwkv_lnx_b,
             moe_w_gate=moe_w_gate, moe_w_up=moe_w_up, moe_w_down=moe_w_down,
             ln2_g3=per_row(ln2_g), ln2_b3=per_row(ln2_b))
    w_in_b = w_in.astype(BF16)
    w_kvt_b = jnp.swapaxes(w_in[:, :, D_A:3 * D_A], 1, 2).astype(BF16)
    w_out_b = w_out.astype(BF16)
    cache_kt, cache_vt = _window_transposed(cache_win_k), _window_transposed(cache_win_v)
    lora = _rwkv_lora_weights(rwkv_w2, rwkv_a2, rwkv_g2)
    wr, br = _router_weights(moe_router1, moe_router1_b, moe_router2, moe_router2_b)
    sgu_bias_full = _expand_sgu_bias(sgu_bias)
    sgu_coef = _sgu_sample_coef(sgu_ws, sgu_bias, ts)
    sgu_g, sgu_b = per_row(sgu_ln_g), per_row(sgu_ln_b)
    shift0 = state_shift[:, :, None, :]

    x_p, x_s, s_first = x_prompt.reshape(n_p, D_MODEL), x_sample.reshape(n_s, D_MODEL), 0
    outs = [[] for _ in range(9)]
    for layer in range(depth):
        q_p, kt_p, vt_p, uv_p, pc_p = _in_proj_prompt(x_p, bp, tp, w_in_b, w_kvt_b, layer)
        qkv_s, uv_s, pc_s = _in_proj_sample(x_s, s_first, n_s, w_in_b, layer)
        vec = _rwkv_vec_table(p, layer)

        oa_p = _attn_prompt(q_p, kt_p, vt_p, bp, tp)
        ob_p = _sgu_prompt(uv_p, n_p, sgu_ws, sgu_bias_full, sgu_g, sgu_b, layer)
        oc_p, wkv_p = _rwkv_prompt(pc_p, bp, tp, lora[layer], vec)

        pc_s = pc_s.reshape(bs, ts, D_C_PROJ)
        oa_s = _attn_sample(qkv_s, cache_kt, cache_vt, layer, bs, ts)
        ob_s, vn_s = _sgu_sample(uv_s, 0, n_s, sgu_coef, sgu_g, sgu_b, layer)
        oc_s, wkv_s = _rwkv_sample(jnp.pad(pc_s, ((0, 0), (0, 8 - ts), (0, 0))), shift0, state_wkv,
                                   lora[layer], vec, layer, ts)
        oc_s = oc_s[:, :ts].reshape(n_s, D_C)

        x1, route = _out_proj((oa_p, ob_p, oc_p), (oa_s, ob_s, oc_s), x_p, x_s, s_first, w_out_b,
                              per_row(ln1_g), per_row(ln1_b), wr, br, layer)
        x = _moe(x1, route, p, layer)
        x_p, x_s, s_first = x, x, n_p

        heads_p = lambda t: jnp.transpose(t.reshape(bp, N_HEADS_A, HEAD_DIM, tp), (0, 3, 1, 2))
        heads_s = lambda t: t.reshape(bs, ts, N_HEADS_A, HEAD_DIM)
        layer_outs = (heads_p(kt_p), heads_p(vt_p), wkv_p,
                      pc_p.reshape(bp, tp, D_C_PROJ)[:, -1],
                      heads_s(qkv_s[:, D_A:2 * D_A]), heads_s(qkv_s[:, 2 * D_A:]), wkv_s, pc_s[:, -1],
                      vn_s.reshape(bs, ts, D_B))
        for acc, o in zip(outs, layer_outs):
            acc.append(o)

    return (x[:n_p].reshape(bp, tp, D_MODEL), x[n_p:].reshape(bs, ts, D_MODEL)) + tuple(jnp.stack(o, 0) for o in outs)
```
